```python
import math
import jax, jax.numpy as jnp
from jax import lax
import numpy as np

D_MODEL = 1024
BATCH = 8
SEQ = 4096
DEPTH = 4

N_MIXERS = 4
N_SSD = (DEPTH + 3) // 4
N_RWKV = (DEPTH + 2) // 4
N_GLA = (DEPTH + 1) // 4
N_RET = DEPTH // 4
CHUNK = 64
NORM_EPS = 1e-5

SSD_D_INNER = 2 * D_MODEL
SSD_HEAD_DIM = 64
SSD_HEADS = SSD_D_INNER // SSD_HEAD_DIM
SSD_GROUPS = 8
SSD_STATE = 128
SSD_CONV = 4
SSD_CONV_DIM = SSD_D_INNER + 2 * SSD_GROUPS * SSD_STATE
SSD_IN_DIM = SSD_D_INNER + SSD_CONV_DIM + SSD_HEADS

RWKV_HEAD_DIM = 64
RWKV_HEADS = D_MODEL // RWKV_HEAD_DIM
RWKV_DECAY_LORA = 64
RWKV_AAA_LORA = 64
RWKV_GATE_LORA = 160
RWKV_GN_EPS = 64e-5

GLA_HEADS = 4
GLA_KEY_DIM = D_MODEL // 2
GLA_DK = GLA_KEY_DIM // GLA_HEADS
GLA_VALUE_DIM = D_MODEL
GLA_DV = GLA_VALUE_DIM // GLA_HEADS
GLA_GATE_RANK = 16
GLA_GATE_NORM = 16.0
GLA_IN_DIM = 2 * GLA_KEY_DIM + 2 * GLA_VALUE_DIM + GLA_GATE_RANK

RET_HEADS = 4
RET_QK_DIM = D_MODEL
RET_DK = RET_QK_DIM // RET_HEADS
RET_V_DIM = 2 * D_MODEL
RET_DV = RET_V_DIM // RET_HEADS
RET_IN_DIM = 2 * RET_QK_DIM + 2 * RET_V_DIM
ROPE_BASE = 10000.0

FFN_HIDDEN = 2816
FFN_CONV = 3

kernel_name = 'hybrid_ssd_rwkv7_gla_retnet_convffn'


def rms_norm(x, w, eps=NORM_EPS):
    xf = x.astype(jnp.float32)
    y = xf * lax.rsqrt(jnp.mean(xf * xf, axis=-1, keepdims=True) + eps)
    return (y * w).astype(x.dtype)


def head_rms(x, eps=NORM_EPS):
    xf = x.astype(jnp.float32)
    return xf * lax.rsqrt(jnp.mean(xf * xf, axis=-1, keepdims=True) + eps)


def causal_dwconv(x, w, b):
    K = w.shape[0]
    y = lax.conv_general_dilated(x, w[:, None, :].astype(x.dtype), window_strides=(1,),
                                 padding=[(K - 1, 0)], dimension_numbers=('NWC', 'WIO', 'NWC'),
                                 feature_group_count=x.shape[-1])
    return y + b


def token_shift(x):
    return jnp.pad(x[:, :-1], ((0, 0), (1, 0), (0, 0)))


def chunked_linear_attn(q, k, v, log_decay):
    out_dtype = v.dtype
    B, T, H, Dv = v.shape
    Hk, Dk = q.shape[2], q.shape[3]
    Dg = log_decay.shape[-1]
    rep = H // Hk
    n = T // CHUNK

    def to_chunks(a):
        return jnp.moveaxis(a.astype(jnp.float32).reshape(B, n, CHUNK, *a.shape[2:]), 1, 0)

    qc, kc, vc = to_chunks(q), to_chunks(k), to_chunks(v)
    gc = jnp.cumsum(to_chunks(log_decay), axis=2)
    causal = jnp.tril(jnp.ones((CHUNK, CHUNK), dtype=bool))

    def step(S, inp):
        qb, kb, vb, gb = inp
        if rep > 1:
            qb = jnp.repeat(qb, rep, axis=2)
            kb = jnp.repeat(kb, rep, axis=2)
        if Dg == 1:
            seg = gb[:, :, None, :, 0] - gb[:, None, :, :, 0]
            dec = jnp.exp(jnp.where(causal[None, :, :, None], seg, -jnp.inf))
            scores = jnp.einsum('blhd,bshd->blsh', qb, kb) * dec
        else:
            seg = gb[:, :, None] - gb[:, None]
            dec = jnp.exp(jnp.where(causal[None, :, :, None, None], seg, -jnp.inf))
            scores = jnp.einsum('blhd,bshd,blshd->blsh', qb, kb, dec)
        o = (jnp.einsum('blsh,bshv->blhv', scores, vb)
             + jnp.einsum('blhd,bhdv->blhv', qb * jnp.exp(gb), S))
        g_last = gb[:, -1]
        S = (S * jnp.exp(g_last)[..., None]
             + jnp.einsum('bshd,bshv->bhdv', kb * jnp.exp(g_last[:, None] - gb), vb))
        return S, o

    S0 = jnp.zeros((B, H, Dk, Dv), jnp.float32)
    _, o = lax.scan(step, S0, (qc, kc, vc, gc))
    return jnp.moveaxis(o, 0, 1).reshape(B, T, H, Dv).astype(out_dtype)


def rwkv7_scan(r, w, k, v, a, b):
    out_dtype = v.dtype
    B, T, H, N = v.shape
    xs = tuple(jnp.moveaxis(t.astype(jnp.float32), 1, 0) for t in (r, w, k, v, a, b))

    def step(S, inp):
        rt, wt, kt, vt, at, bt = inp
        sa = jnp.einsum('bhij,bhj->bhi', S, at)
        S = S * wt[:, :, None, :] + sa[..., None] * bt[:, :, None, :] + vt[..., None] * kt[:, :, None, :]
        return S, jnp.einsum('bhij,bhj->bhi', S, rt)

    S0 = jnp.zeros((B, H, N, N), jnp.float32)
    _, y = lax.scan(step, S0, xs)
    return jnp.moveaxis(y, 0, 1).astype(out_dtype)


def ssd_mixer(x, w_in, conv_w, conv_b, dt_bias, a_log, d_skip, norm_w, w_out):
    B, T, _ = x.shape
    zxbcdt = x @ w_in
    z, xbc, dt = jnp.split(zxbcdt, [SSD_D_INNER, SSD_D_INNER + SSD_CONV_DIM], axis=-1)
    xbc = jax.nn.silu(causal_dwconv(xbc, conv_w, conv_b))
    xs, bm, cm = jnp.split(xbc, [SSD_D_INNER, SSD_D_INNER + SSD_GROUPS * SSD_STATE], axis=-1)
    xs = xs.reshape(B, T, SSD_HEADS, SSD_HEAD_DIM)
    bm = bm.reshape(B, T, SSD_GROUPS, SSD_STATE)
    cm = cm.reshape(B, T, SSD_GROUPS, SSD_STATE)
    dt = jax.nn.softplus(dt.astype(jnp.float32) + dt_bias)
    a = -jnp.exp(a_log.astype(jnp.float32))
    y = chunked_linear_attn(cm, bm, xs * dt[..., None].astype(xs.dtype), (dt * a)[..., None])
    y = y + d_skip[:, None] * xs
    y = y.reshape(B, T, SSD_D_INNER) * jax.nn.silu(z)
    y = head_rms(y.reshape(B, T, SSD_GROUPS, SSD_D_INNER // SSD_GROUPS)).reshape(B, T, SSD_D_INNER)
    return (y * norm_w).astype(x.dtype) @ w_out


def rwkv7_mixer(x, mix, w_rkv, w0, w1, w2, a0, a1, a2, g1, g2, k_k, k_a, r_k, ln_w, ln_b, w_out):
    B, T, D = x.shape
    H, N = RWKV_HEADS, RWKV_HEAD_DIM
    xx = token_shift(x) - x
    xr, xw, xk, xv, xa, xg = [x + xx * mix[i] for i in range(6)]
    r = xr @ w_rkv[0]
    k = xk @ w_rkv[1]
    v = xv @ w_rkv[2]
    w_raw = -jax.nn.softplus(-(w0 + jnp.tanh(xw @ w1) @ w2).astype(jnp.float32)) - 0.5
    decay = jnp.exp(-jnp.exp(w_raw))
    a = jax.nn.sigmoid(a0 + (xa @ a1) @ a2)
    g = jax.nn.sigmoid(xg @ g1) @ g2
    kk = (k * k_k).reshape(B, T, H, N).astype(jnp.float32)
    kk = kk / jnp.maximum(jnp.sqrt(jnp.sum(kk * kk, axis=-1, keepdims=True)), 1e-12)
    k = k * (1.0 + (a - 1.0) * k_a)
    ah = a.reshape(B, T, H, N)
    rh, kh, vh = r.reshape(B, T, H, N), k.reshape(B, T, H, N), v.reshape(B, T, H, N)
    y = rwkv7_scan(rh, decay.reshape(B, T, H, N), kh, vh, -kk, kk * ah).astype(jnp.float32)
    mu = jnp.mean(y, axis=-1, keepdims=True)
    var = jnp.mean(jnp.square(y - mu), axis=-1, keepdims=True)
    y = ((y - mu) * lax.rsqrt(var + RWKV_GN_EPS)).reshape(B, T, D) * ln_w + ln_b
    bonus = jnp.sum(rh * kh * r_k, axis=-1, keepdims=True) * vh
    y = y + bonus.reshape(B, T, D)
    return (y * g).astype(x.dtype) @ w_out


def gla_mixer(x, w_in, w_gk2, b_gk2, norm_w, w_out):
    B, T, _ = x.shape
    proj = x @ w_in
    q, k, v, g, gk = jnp.split(proj, [GLA_KEY_DIM, 2 * GLA_KEY_DIM, 2 * GLA_KEY_DIM + GLA_VALUE_DIM,
                                     2 * GLA_KEY_DIM + 2 * GLA_VALUE_DIM], axis=-1)
    log_a = jax.nn.log_sigmoid((gk @ w_gk2 + b_gk2).astype(jnp.float32)) / GLA_GATE_NORM
    q = q.reshape(B, T, GLA_HEADS, GLA_DK) * (GLA_DK ** -0.5)
    k = k.reshape(B, T, GLA_HEADS, GLA_DK)
    v = v.reshape(B, T, GLA_HEADS, GLA_DV)
    o = chunked_linear_attn(q, k, v, log_a.reshape(B, T, GLA_HEADS, GLA_DK))
    o = (head_rms(o) * norm_w).reshape(B, T, GLA_VALUE_DIM) * jax.nn.silu(g.astype(jnp.float32))
    return o.astype(x.dtype) @ w_out


def retnet_rotary(x):
    T, Dk = x.shape[1], x.shape[-1]
    inv = 1.0 / (ROPE_BASE ** jnp.linspace(0.0, 1.0, Dk // 2, dtype=jnp.float32))
    ang = jnp.repeat(jnp.arange(T, dtype=jnp.float32)[:, None] * inv[None], 2, axis=-1)
    sin, cos = jnp.sin(ang)[None, :, None], jnp.cos(ang)[None, :, None]
    xf = x.astype(jnp.float32)
    rot = jnp.stack([-xf[..., 1::2], xf[..., 0::2]], axis=-1).reshape(xf.shape)
    return (xf * cos + rot * sin).astype(x.dtype)


def retention_mixer(x, w_in, w_out):
    B, T, _ = x.shape
    proj = x @ w_in
    q, k, v, g = jnp.split(proj, [RET_QK_DIM, 2 * RET_QK_DIM, 2 * RET_QK_DIM + RET_V_DIM], axis=-1)
    q = retnet_rotary(q.reshape(B, T, RET_HEADS, RET_DK))
    k = retnet_rotary(k.reshape(B, T, RET_HEADS, RET_DK)) * (RET_DK ** -0.5)
    v = v.reshape(B, T, RET_HEADS, RET_DV)
    log_gamma = jnp.log(1.0 - 2.0 ** (-5.0 - jnp.arange(RET_HEADS, dtype=jnp.float32)))
    log_decay = jnp.broadcast_to(log_gamma[None, None, :, None], (B, T, RET_HEADS, 1))
    o = head_rms(chunked_linear_attn(q, k, v, log_decay)).reshape(B, T, RET_V_DIM)
    return (jax.nn.silu(g.astype(jnp.float32)) * o).astype(x.dtype) @ w_out


def conv_ffn(x, w_up, conv_w, conv_b, w_down):
    h = causal_dwconv(x @ w_up, conv_w, conv_b)
    val, gate = jnp.split(h, 2, axis=-1)
    return (jax.nn.silu(gate) * val) @ w_down


def setup_inputs(seed: int = 0) -> dict:
    key = jax.random.key(seed)
    ks = iter(jax.random.split(key, 64))

    def nrm(shape, scale):
        return scale * jax.random.normal(next(ks), shape, jnp.float32)

    def uni(shape, lo, hi):
        return jax.random.uniform(next(ks), shape, jnp.float32, lo, hi)

    D = D_MODEL
    F2 = 2 * FFN_HIDDEN
    dt0 = jnp.exp(uni((N_SSD, SSD_HEADS), math.log(1e-3), math.log(1e-1)))
    return {
        'x': nrm((BATCH, SEQ, D), 1.0),
        'norm_mix': 1.0 + nrm((DEPTH, D), 0.02),
        'norm_ffn': 1.0 + nrm((DEPTH, D), 0.02),
        'norm_final': 1.0 + nrm((D,), 0.02),
        'ssd_w_in': nrm((N_SSD, D, SSD_IN_DIM), D ** -0.5),
        'ssd_conv_w': nrm((N_SSD, SSD_CONV, SSD_CONV_DIM), SSD_CONV ** -0.5),
        'ssd_conv_b': nrm((N_SSD, SSD_CONV_DIM), 0.02),
        'ssd_dt_bias': dt0 + jnp.log(-jnp.expm1(-dt0)),
        'ssd_a_log': jnp.log(uni((N_SSD, SSD_HEADS), 1.0, 16.0)),
        'ssd_d': 1.0 + nrm((N_SSD, SSD_HEADS), 0.1),
        'ssd_norm_w': 1.0 + nrm((N_SSD, SSD_D_INNER), 0.02),
        'ssd_w_out': nrm((N_SSD, SSD_D_INNER, D), SSD_D_INNER ** -0.5),
        'rwkv_mix': uni((N_RWKV, 6, D), 0.0, 1.0),
        'rwkv_w_rkv': nrm((N_RWKV, 3, D, D), D ** -0.5),
        'rwkv_w0': uni((N_RWKV, D), -6.0, -1.0),
        'rwkv_w1': nrm((N_RWKV, D, RWKV_DECAY_LORA), D ** -0.5),
        'rwkv_w2': nrm((N_RWKV, RWKV_DECAY_LORA, D), 0.1 * RWKV_DECAY_LORA ** -0.5),
        'rwkv_a0': nrm((N_RWKV, D), 0.1),
        'rwkv_a1': nrm((N_RWKV, D, RWKV_AAA_LORA), D ** -0.5),
        'rwkv_a2': nrm((N_RWKV, RWKV_AAA_LORA, D), RWKV_AAA_LORA ** -0.5),
        'rwkv_g1': nrm((N_RWKV, D, RWKV_GATE_LORA), D ** -0.5),
        'rwkv_g2': nrm((N_RWKV, RWKV_GATE_LORA, D), RWKV_GATE_LORA ** -0.5),
        'rwkv_k_k': 0.85 + nrm((N_RWKV, D), 0.05),
        'rwkv_k_a': 1.0 + nrm((N_RWKV, D), 0.05),
        'rwkv_r_k': nrm((N_RWKV, RWKV_HEADS, RWKV_HEAD_DIM), 0.1),
        'rwkv_ln_w': 1.0 + nrm((N_RWKV, D), 0.02),
        'rwkv_ln_b': nrm((N_RWKV, D), 0.02),
        'rwkv_w_out': nrm((N_RWKV, D, D), D ** -0.5),
        'gla_w_in': nrm((N_GLA, D, GLA_IN_DIM), D ** -0.5),
        'gla_w_gk2': nrm((N_GLA, GLA_GATE_RANK, GLA_KEY_DIM), GLA_GATE_RANK ** -0.5),
        'gla_b_gk2': nrm((N_GLA, GLA_KEY_DIM), 0.1),
        'gla_norm_w': 1.0 + nrm((N_GLA, GLA_DV), 0.02),
        'gla_w_out': nrm((N_GLA, GLA_VALUE_DIM, D), GLA_VALUE_DIM ** -0.5),
        'ret_w_in': nrm((N_RET, D, RET_IN_DIM), D ** -0.5),
        'ret_w_out': nrm((N_RET, RET_V_DIM, D), RET_V_DIM ** -0.5),
        'ffn_w_up': nrm((DEPTH, D, F2), D ** -0.5),
        'ffn_conv_w': nrm((DEPTH, FFN_CONV, F2), FFN_CONV ** -0.5),
        'ffn_conv_b': nrm((DEPTH, F2), 0.02),
        'ffn_w_down': nrm((DEPTH, FFN_HIDDEN, D), FFN_HIDDEN ** -0.5),
    }


def reference(x, norm_mix, norm_ffn, norm_final,
              ssd_w_in, ssd_conv_w, ssd_conv_b, ssd_dt_bias, ssd_a_log, ssd_d, ssd_norm_w, ssd_w_out,
              rwkv_mix, rwkv_w_rkv, rwkv_w0, rwkv_w1, rwkv_w2, rwkv_a0, rwkv_a1, rwkv_a2,
              rwkv_g1, rwkv_g2, rwkv_k_k, rwkv_k_a, rwkv_r_k, rwkv_ln_w, rwkv_ln_b, rwkv_w_out,
              gla_w_in, gla_w_gk2, gla_b_gk2, gla_norm_w, gla_w_out,
              ret_w_in, ret_w_out,
              ffn_w_up, ffn_conv_w, ffn_conv_b, ffn_w_down):
    for i in range(DEPTH):
        m, j = i % N_MIXERS, i // N_MIXERS
        h = rms_norm(x, norm_mix[i])
        if m == 0:
            h = ssd_mixer(h, ssd_w_in[j], ssd_conv_w[j], ssd_conv_b[j], ssd_dt_bias[j], ssd_a_log[j],
                          ssd_d[j], ssd_norm_w[j], ssd_w_out[j])
        elif m == 1:
            h = rwkv7_mixer(h, rwkv_mix[j], rwkv_w_rkv[j], rwkv_w0[j], rwkv_w1[j], rwkv_w2[j],
                            rwkv_a0[j], rwkv_a1[j], rwkv_a2[j], rwkv_g1[j], rwkv_g2[j],
                            rwkv_k_k[j], rwkv_k_a[j], rwkv_r_k[j], rwkv_ln_w[j], rwkv_ln_b[j],
                            rwkv_w_out[j])
        elif m == 2:
            h = gla_mixer(h, gla_w_in[j], gla_w_gk2[j], gla_b_gk2[j], gla_norm_w[j], gla_w_out[j])
        else:
            h = retention_mixer(h, ret_w_in[j], ret_w_out[j])
        x = x + h
        x = x + conv_ffn(rms_norm(x, norm_ffn[i]), ffn_w_up[i], ffn_conv_w[i], ffn_conv_b[i], ffn_w_down[i])
    return rms_norm(x, norm_final)
```

```python
import functools
import math

import jax
import jax.numpy as jnp
from jax import lax
from jax.experimental import pallas as pl
from jax.experimental.pallas import tpu as pltpu

D_MODEL = 1024
NORM_EPS = 1e-5
CHUNK = 128

SSD_D_INNER = 2048
SSD_HEAD_DIM = 64
SSD_HEADS = 32
SSD_GROUPS = 8
SSD_STATE = 128
SSD_CONV = 4
SSD_CONV_DIM = SSD_D_INNER + 2 * SSD_GROUPS * SSD_STATE
SSD_HEADS_PER_GROUP = SSD_HEADS // SSD_GROUPS
SSD_GROUP_V = SSD_HEADS_PER_GROUP * SSD_HEAD_DIM

RWKV_HEADS = 16
RWKV_HEAD_DIM = 64
RWKV_GN_EPS = 64e-5

GLA_HEADS = 4
GLA_DK = 128
GLA_DV = 256
GLA_KEY_DIM = 512
GLA_VALUE_DIM = 1024
GLA_GATE_RANK = 16
GLA_GATE_NORM = 16.0
GLA_SUB = 16

RET_HEADS = 4
RET_DK = 256
RET_DV = 512
RET_QK_DIM = 1024
RET_V_DIM = 2048
ROPE_BASE = 10000.0

FFN_HIDDEN = 2816
FFN_CONV = 3

V7X_LANES = 128
V7X_BF16_SUBLANES = 16
V7X_F32_SUBLANES = 8
V7X_VMEM_LIMIT_BYTES = 56 * 1024 * 1024

F32 = jnp.float32
BF16 = jnp.bfloat16
NEG_BIG = -1e30


def _cparams(sem):
    return pltpu.CompilerParams(dimension_semantics=sem, vmem_limit_bytes=V7X_VMEM_LIMIT_BYTES)


def _rms(x, w):
    return x * lax.rsqrt(jnp.mean(x * x, axis=-1, keepdims=True) + NORM_EPS) * w


def _sigmoid(x):
    return 1.0 / (1.0 + jnp.exp(-x))


def _silu(x):
    return x * _sigmoid(x)


def _softplus(x):
    return jnp.maximum(x, 0.0) + jnp.log(1.0 + jnp.exp(-jnp.abs(x)))


def _dot(a, b):
    return jnp.dot(a, b, preferred_element_type=F32)


def _dot_nt(a, b):
    return lax.dot_general(a, b, (((1,), (1,)), ((), ())), preferred_element_type=F32)


def _dot_tn(a, b):
    return lax.dot_general(a, b, (((0,), (0,)), ((), ())), preferred_element_type=F32)


def _split_dot(t_bf16, x_f32):
    hi = x_f32.astype(BF16)
    lo = (x_f32 - hi.astype(F32)).astype(BF16)
    return _dot(t_bf16, hi) + _dot(t_bf16, lo)


def _tril_ones(n, dtype):
    r = lax.broadcasted_iota(jnp.int32, (n, n), 0)
    c = lax.broadcasted_iota(jnp.int32, (n, n), 1)
    return jnp.where(r >= c, 1.0, 0.0).astype(dtype)


def _norm_matmul_kernel(x_ref, nw_ref, w_ref, o_ref, xn_ref):
    @pl.when(pl.program_id(1) == 0)
    def _():
        xn_ref[...] = _rms(x_ref[...], nw_ref[...]).astype(BF16)

    o_ref[...] = _dot(xn_ref[...], w_ref[...]).astype(o_ref.dtype)


def norm_matmul(x2, norm_w, w_bf16, out_dtype, tn):
    m, d = x2.shape
    n = w_bf16.shape[1]
    tm = min(1024, m)
    return pl.pallas_call(
        _norm_matmul_kernel,
        grid=(m // tm, n // tn),
        in_specs=[
            pl.BlockSpec((tm, d), lambda i, j: (i, 0)),
            pl.BlockSpec((1, d), lambda i, j: (0, 0)),
            pl.BlockSpec((d, tn), lambda i, j: (0, j)),
        ],
        out_specs=pl.BlockSpec((tm, tn), lambda i, j: (i, j)),
        out_shape=jax.ShapeDtypeStruct((m, n), out_dtype),
        scratch_shapes=[pltpu.VMEM((tm, d), BF16)],
        compiler_params=_cparams(("parallel", "arbitrary")),
        name="norm_matmul",
    )(x2, norm_w.reshape(1, d), w_bf16)


def _gate_matmul_res_kernel(x_ref, y_ref, g_ref, w_ref, o_ref):
    yg = (y_ref[...].astype(F32) * g_ref[...].astype(F32)).astype(BF16)
    o_ref[...] = x_ref[...] + _dot(yg, w_ref[...])


def gate_matmul_res(x2, y2, g2, w_bf16):
    m, d = x2.shape
    k = y2.shape[1]
    tm = min(1024, m)
    return pl.pallas_call(
        _gate_matmul_res_kernel,
        grid=(m // tm,),
        in_specs=[
            pl.BlockSpec((tm, d), lambda i: (i, 0)),
            pl.BlockSpec((tm, k), lambda i: (i, 0)),
            pl.BlockSpec((tm, k), lambda i: (i, 0)),
            pl.BlockSpec((k, d), lambda i: (0, 0)),
        ],
        out_specs=pl.BlockSpec((tm, d), lambda i: (i, 0)),
        out_shape=jax.ShapeDtypeStruct((m, d), F32),
        compiler_params=_cparams(("parallel",)),
        name="gate_matmul_res",
    )(x2, y2, g2, w_bf16)


FFN_HALO = V7X_BF16_SUBLANES


def _ffn_kernel(x_ref, xh_ref, nw_ref, wv_ref, wg_ref, cwv_ref, cwg_ref, cbv_ref, cbg_ref,
                wd_ref, fnw_ref, o_ref, xn_ref, hv_ref, hg_ref, *, tm, tiles_per_seq, final_norm):
    i = pl.program_id(0)
    j = pl.program_id(1)

    @pl.when(j == 0)
    def _():
        x = x_ref[...]
        xn_ref[pl.ds(FFN_HALO, tm), :] = _rms(x, nw_ref[...]).astype(BF16)
        keep = jnp.where(i % tiles_per_seq == 0, 0.0, 1.0)
        xn_ref[pl.ds(0, FFN_HALO), :] = (_rms(xh_ref[...], nw_ref[...]) * keep).astype(BF16)
        o_ref[...] = x

    xn = xn_ref[...]
    hv_ref[...] = _dot(xn, wv_ref[...])
    hg_ref[...] = _dot(xn, wg_ref[...])

    def conv(h_ref, cw_ref, cb_ref):
        return (cw_ref[2:3, :] * h_ref[pl.ds(FFN_HALO, tm), :]
                + cw_ref[1:2, :] * h_ref[pl.ds(FFN_HALO - 1, tm), :]
                + cw_ref[0:1, :] * h_ref[pl.ds(FFN_HALO - 2, tm), :]
                + cb_ref[...])

    val = conv(hv_ref, cwv_ref, cbv_ref)
    gate = conv(hg_ref, cwg_ref, cbg_ref)
    act = (val * _silu(gate)).astype(BF16)
    o_ref[...] += _dot(act, wd_ref[...])

    if final_norm:
        @pl.when(j == pl.num_programs(1) - 1)
        def _():
            o_ref[...] = _rms(o_ref[...], fnw_ref[...])


def ffn(x2, seq_len, norm_w, w_up_bf16, conv_w, conv_b, w_down_bf16, final_norm_w, final_norm):
    m, d = x2.shape
    f = FFN_HIDDEN
    tm = min(1024, seq_len)
    tf = 256
    nf = f // tf
    tiles_per_seq = seq_len // tm
    halo_blocks = tm // FFN_HALO
    kern = functools.partial(_ffn_kernel, tm=tm, tiles_per_seq=tiles_per_seq, final_norm=final_norm)
    return pl.pallas_call(
        kern,
        grid=(m // tm, nf),
        in_specs=[
            pl.BlockSpec((tm, d), lambda i, j: (i, 0)),
            pl.BlockSpec((FFN_HALO, d), lambda i, j: (jnp.maximum(i * halo_blocks - 1, 0), 0)),
            pl.BlockSpec((1, d), lambda i, j: (0, 0)),
            pl.BlockSpec((d, tf), lambda i, j: (0, j)),
            pl.BlockSpec((d, tf), lambda i, j: (0, nf + j)),
            pl.BlockSpec((FFN_CONV, tf), lambda i, j: (0, j)),
            pl.BlockSpec((FFN_CONV, tf), lambda i, j: (0, nf + j)),
            pl.BlockSpec((1, tf), lambda i, j: (0, j)),
            pl.BlockSpec((1, tf), lambda i, j: (0, nf + j)),
            pl.BlockSpec((tf, d), lambda i, j: (j, 0)),
            pl.BlockSpec((1, d), lambda i, j: (0, 0)),
        ],
        out_specs=pl.BlockSpec((tm, d), lambda i, j: (i, 0)),
        out_shape=jax.ShapeDtypeStruct((m, d), F32),
        scratch_shapes=[
            pltpu.VMEM((tm + FFN_HALO, d), BF16),
            pltpu.VMEM((tm + FFN_HALO, tf), F32),
            pltpu.VMEM((tm + FFN_HALO, tf), F32),
        ],
        compiler_params=_cparams(("parallel", "arbitrary")),
        name="ffn",
    )(x2, x2, norm_w.reshape(1, d), w_up_bf16, w_up_bf16, conv_w, conv_w,
      conv_b.reshape(1, 2 * f), conv_b.reshape(1, 2 * f), w_down_bf16, final_norm_w.reshape(1, d))


SSD_HALO = V7X_F32_SUBLANES


def _ssd_kernel(proj_ref, dt_ref, x_ref, cw_ref, cb_ref, dtb_ref, alog_ref, dexp_ref, nw_ref,
                wout_ref, e_ref, o_ref, s_ref, halo_ref, ext_ref, act_ref, yn_ref):
    L = CHUNK
    c = pl.program_id(1)

    @pl.when(c == 0)
    def _():
        s_ref[...] = jnp.zeros_like(s_ref)
        halo_ref[...] = jnp.zeros_like(halo_ref)

    ext_ref[pl.ds(0, SSD_HALO), :] = halo_ref[...]
    ext_ref[pl.ds(SSD_HALO, L), :] = proj_ref[0, :, pl.ds(SSD_D_INNER, SSD_CONV_DIM)].astype(F32)
    halo_ref[...] = ext_ref[pl.ds(L, SSD_HALO), :]
    panel = 512
    for p in range(SSD_CONV_DIM // panel):
        cs = pl.ds(p * panel, panel)
        acc = cb_ref[:, cs] + cw_ref[3:4, cs] * ext_ref[pl.ds(SSD_HALO, L), cs]
        for tap in range(SSD_CONV - 1):
            acc = acc + cw_ref[tap:tap + 1, cs] * ext_ref[pl.ds(SSD_HALO - 3 + tap, L), cs]
        act_ref[:, cs] = _silu(acc)

    dt = _softplus(dt_ref[0] + dtb_ref[...])
    a = -jnp.exp(alog_ref[...])
    tril_b = _tril_ones(L, BF16)
    cum = _split_dot(tril_b, dt * a)
    cum_last = cum[L - 1:L, :]
    cum_t = cum.T
    dt_t = dt.T
    e_in = jnp.exp(cum)
    c_st = dt * jnp.exp(cum_last - cum)
    g_last = jnp.broadcast_to(jnp.exp(cum_last), (V7X_F32_SUBLANES, V7X_LANES))
    stacked = jnp.concatenate([e_in, c_st, g_last], axis=0).astype(BF16)
    expanded = _dot(stacked, e_ref[...])
    e_exp = expanded[0:L]
    c_exp = expanded[L:2 * L]
    g_exp = expanded[2 * L:2 * L + 1]

    row = lax.broadcasted_iota(jnp.int32, (L, L), 0)
    col = lax.broadcasted_iota(jnp.int32, (L, L), 1)
    causal = row >= col
    vcol = lax.broadcasted_iota(jnp.int32, (L, SSD_GROUP_V), 1) // SSD_HEAD_DIM

    for g in range(SSD_GROUPS):
        vs = pl.ds(g * SSD_GROUP_V, SSD_GROUP_V)
        kq = act_ref[:, pl.ds(SSD_D_INNER + g * SSD_STATE, SSD_STATE)].astype(BF16)
        qq = act_ref[:, pl.ds(SSD_D_INNER + (SSD_GROUPS + g) * SSD_STATE, SSD_STATE)].astype(BF16)
        xs = act_ref[:, vs]
        sb = _dot_nt(qq, kq)
        s_old = s_ref[g]
        q_s = _dot(qq, s_old.astype(BF16))
        probs = []
        vblk = []
        for hh in range(SSD_HEADS_PER_GROUP):
            h = g * SSD_HEADS_PER_GROUP + hh
            seg = cum[:, h:h + 1] - cum_t[h:h + 1, :]
            dec = jnp.exp(jnp.where(causal, seg, NEG_BIG)) * dt_t[h:h + 1, :]
            probs.append((sb * dec).astype(BF16))
            vblk.append(jnp.where(vcol == hh, xs, 0.0).astype(BF16))
        p_cat = jnp.concatenate(probs, axis=1)
        v_blk = jnp.concatenate(vblk, axis=0)
        y = _dot(p_cat, v_blk) + e_exp[:, g * SSD_GROUP_V:(g + 1) * SSD_GROUP_V] * q_s
        xs_scaled = (xs * c_exp[:, g * SSD_GROUP_V:(g + 1) * SSD_GROUP_V]).astype(BF16)
        s_ref[g] = s_old * g_exp[:, g * SSD_GROUP_V:(g + 1) * SSD_GROUP_V] + _dot_tn(kq, xs_scaled)
        y = y + dexp_ref[:, vs] * xs
        y = y * _silu(proj_ref[0, :, vs].astype(F32))
        y = y * lax.rsqrt(jnp.mean(y * y, axis=-1, keepdims=True) + NORM_EPS)
        yn_ref[:, vs] = (y * nw_ref[:, vs]).astype(BF16)

    o_ref[0] = x_ref[0] + _dot(yn_ref[...], wout_ref[...])


def ssd_core(proj3, dt3, x3, conv_w, conv_b, dt_bias, a_log, d_skip, norm_w, w_out_bf16):
    b, t, d = x3.shape
    L = CHUNK
    pad = V7X_LANES - SSD_HEADS
    dtb = jnp.pad(dt_bias, (0, pad)).reshape(1, V7X_LANES)
    alog = jnp.pad(a_log, (0, pad)).reshape(1, V7X_LANES)
    dexp = jnp.repeat(d_skip, SSD_HEAD_DIM).reshape(1, SSD_D_INNER)
    expand = (jnp.arange(V7X_LANES)[:, None] == (jnp.arange(SSD_D_INNER)[None, :] // SSD_HEAD_DIM)).astype(BF16)
    pw = proj3.shape[2]
    const = lambda bi, ci: (0, 0)
    return pl.pallas_call(
        _ssd_kernel,
        grid=(b, t // L),
        in_specs=[
            pl.BlockSpec((1, L, pw), lambda bi, ci: (bi, ci, 0)),
            pl.BlockSpec((1, L, V7X_LANES), lambda bi, ci: (bi, ci, 0)),
            pl.BlockSpec((1, L, d), lambda bi, ci: (bi, ci, 0)),
            pl.BlockSpec((SSD_CONV, SSD_CONV_DIM), const),
            pl.BlockSpec((1, SSD_CONV_DIM), const),
            pl.BlockSpec((1, V7X_LANES), const),
            pl.BlockSpec((1, V7X_LANES), const),
            pl.BlockSpec((1, SSD_D_INNER), const),
            pl.BlockSpec((1, SSD_D_INNER), const),
            pl.BlockSpec((SSD_D_INNER, d), const),
            pl.BlockSpec((V7X_LANES, SSD_D_INNER), const),
        ],
        out_specs=pl.BlockSpec((1, L, d), lambda bi, ci: (bi, ci, 0)),
        out_shape=jax.ShapeDtypeStruct((b, t, d), F32),
        scratch_shapes=[
            pltpu.VMEM((SSD_GROUPS, SSD_STATE, SSD_GROUP_V), F32),
            pltpu.VMEM((SSD_HALO, SSD_CONV_DIM), F32),
            pltpu.VMEM((L + SSD_HALO, SSD_CONV_DIM), F32),
            pltpu.VMEM((L, SSD_CONV_DIM), F32),
            pltpu.VMEM((L, SSD_D_INNER), BF16),
        ],
        compiler_params=_cparams(("parallel", "arbitrary")),
        name="ssd_core",
    )(proj3, dt3, x3, conv_w, conv_b.reshape(1, SSD_CONV_DIM), dtb, alog, dexp,
      norm_w.reshape(1, SSD_D_INNER), w_out_bf16, expand)


def _gla_kernel(proj_ref, x_ref, wgk_ref, bgk_ref, nw_ref, wout_ref, o_ref, s_ref, yn_ref):
    L = CHUNK
    c = pl.program_id(1)

    @pl.when(c == 0)
    def _():
        s_ref[...] = jnp.zeros_like(s_ref)

    gk_off = 2 * GLA_KEY_DIM + 2 * GLA_VALUE_DIM
    u = _dot(proj_ref[0, :, pl.ds(gk_off, V7X_LANES)], wgk_ref[...]) + bgk_ref[...]
    log_a = (jnp.minimum(u, 0.0) - jnp.log(1.0 + jnp.exp(-jnp.abs(u)))) * (1.0 / GLA_GATE_NORM)
    gcum = _split_dot(_tril_ones(L, BF16), log_a)

    row = lax.broadcasted_iota(jnp.int32, (L, L), 0)
    col = lax.broadcasted_iota(jnp.int32, (L, L), 1)
    causal = row >= col
    krow = lax.broadcasted_iota(jnp.int32, (L, GLA_DK), 0)
    nsub = L // GLA_SUB

    for h in range(GLA_HEADS):
        ks = pl.ds(h * GLA_DK, GLA_DK)
        q = proj_ref[0, :, ks].astype(F32) * (GLA_DK ** -0.5)
        k = proj_ref[0, :, pl.ds(GLA_KEY_DIM + h * GLA_DK, GLA_DK)].astype(F32)
        v = proj_ref[0, :, pl.ds(2 * GLA_KEY_DIM + h * GLA_DV, GLA_DV)]
        gh = gcum[:, h * GLA_DK:(h + 1) * GLA_DK]
        blocks = []
        for i in range(nsub):
            r0 = i * GLA_SUB
            if i == 0:
                g_ref_row = jnp.zeros((1, GLA_DK), F32)
            else:
                g_ref_row = gh[r0 - 1:r0, :]
            q_i = (q[r0:r0 + GLA_SUB] * jnp.exp(gh[r0:r0 + GLA_SUB] - g_ref_row)).astype(BF16)
            k_i = (k * jnp.exp(jnp.where(krow < r0 + GLA_SUB, g_ref_row - gh, NEG_BIG))).astype(BF16)
            blocks.append(_dot_nt(q_i, k_i))
        scores = jnp.where(causal, jnp.concatenate(blocks, axis=0), 0.0).astype(BF16)
        st_old = s_ref[h]
        o = _dot(scores, v) + _dot_nt((q * jnp.exp(gh)).astype(BF16), st_old.astype(BF16))
        g_last = gh[L - 1:L, :]
        k_hat = (k * jnp.exp(g_last - gh)).astype(BF16)
        s_ref[h] = st_old * jnp.exp(g_last) + _dot_tn(v, k_hat)
        o = o * lax.rsqrt(jnp.mean(o * o, axis=-1, keepdims=True) + NORM_EPS) * nw_ref[...]
        gate = proj_ref[0, :, pl.ds(2 * GLA_KEY_DIM + GLA_VALUE_DIM + h * GLA_DV, GLA_DV)].astype(F32)
        yn_ref[:, pl.ds(h * GLA_DV, GLA_DV)] = (o * _silu(gate)).astype(BF16)

    o_ref[0] = x_ref[0] + _dot(yn_ref[...], wout_ref[...])


def gla_core(proj3, x3, w_gk2, b_gk2, norm_w, w_out_bf16):
    b, t, d = x3.shape
    L = CHUNK
    pw = proj3.shape[2]
    wgk = jnp.pad(w_gk2, ((0, V7X_LANES - GLA_GATE_RANK), (0, 0))).astype(BF16)
    const = lambda bi, ci: (0, 0)
    return pl.pallas_call(
        _gla_kernel,
        grid=(b, t // L),
        in_specs=[
            pl.BlockSpec((1, L, pw), lambda bi, ci: (bi, ci, 0)),
            pl.BlockSpec((1, L, d), lambda bi, ci: (bi, ci, 0)),
            pl.BlockSpec((V7X_LANES, GLA_KEY_DIM), const),
            pl.BlockSpec((1, GLA_KEY_DIM), const),
            pl.BlockSpec((1, GLA_DV), const),
            pl.BlockSpec((GLA_VALUE_DIM, d), const),
        ],
        out_specs=pl.BlockSpec((1, L, d), lambda bi, ci: (bi, ci, 0)),
        out_shape=jax.ShapeDtypeStruct((b, t, d), F32),
        scratch_shapes=[
            pltpu.VMEM((GLA_HEADS, GLA_DV, GLA_DK), F32),
            pltpu.VMEM((L, GLA_VALUE_DIM), BF16),
        ],
        compiler_params=_cparams(("parallel", "arbitrary")),
        name="gla_core",
    )(proj3, x3, wgk, b_gk2.reshape(1, GLA_KEY_DIM), norm_w.reshape(1, GLA_DV), w_out_bf16)


def _ret_kernel(proj_ref, x_ref, cos_ref, sin_ref, wout_ref, o_ref, s_ref, yn_ref):
    L = CHUNK
    c = pl.program_id(1)

    @pl.when(c == 0)
    def _():
        s_ref[...] = jnp.zeros_like(s_ref)

    cos = cos_ref[...]
    sin = sin_ref[...]
    half = RET_DK // 2

    def rotary(xh):
        xe = xh[:, :half]
        xo = xh[:, half:]
        return jnp.concatenate([xe * cos - xo * sin, xo * cos + xe * sin], axis=1)

    rowf = lax.broadcasted_iota(jnp.int32, (L, L), 0).astype(F32)
    colf = lax.broadcasted_iota(jnp.int32, (L, L), 1).astype(F32)
    diff = rowf - colf
    trow = lax.broadcasted_iota(jnp.int32, (L, V7X_LANES), 0).astype(F32)

    for h in range(RET_HEADS):
        lg = math.log(1.0 - 2.0 ** (-5.0 - h))
        q = rotary(proj_ref[0, :, pl.ds(h * RET_DK, RET_DK)].astype(F32))
        k = rotary(proj_ref[0, :, pl.ds(RET_QK_DIM + h * RET_DK, RET_DK)].astype(F32)) * (RET_DK ** -0.5)
        v = proj_ref[0, :, pl.ds(2 * RET_QK_DIM + h * RET_DV, RET_DV)]
        qb = q.astype(BF16)
        dec = jnp.exp(jnp.where(diff >= 0.0, diff * lg, NEG_BIG))
        p = (_dot_nt(qb, k.astype(BF16)) * dec).astype(BF16)
        e_in = jnp.exp((trow + 1.0) * lg)
        e_in = jnp.concatenate([e_in] * (RET_DV // V7X_LANES), axis=1)
        s_old = s_ref[h]
        o = _dot(p, v) + e_in * _dot(qb, s_old.astype(BF16))
        e_out = jnp.exp((float(L - 1) - trow) * lg)
        k_hat = (k * jnp.concatenate([e_out] * (RET_DK // V7X_LANES), axis=1)).astype(BF16)
        s_ref[h] = s_old * math.exp(L * lg) + _dot_tn(k_hat, v)
        o = o * lax.rsqrt(jnp.mean(o * o, axis=-1, keepdims=True) + NORM_EPS)
        gate = proj_ref[0, :, pl.ds(2 * RET_QK_DIM + RET_V_DIM + h * RET_DV, RET_DV)].astype(F32)
        yn_ref[:, pl.ds(h * RET_DV, RET_DV)] = (_silu(gate) * o).astype(BF16)

    o_ref[0] = x_ref[0] + _dot(yn_ref[...], wout_ref[...])


def ret_core(proj3, x3, cos, sin, w_out_bf16):
    b, t, d = x3.shape
    L = CHUNK
    pw = proj3.shape[2]
    return pl.pallas_call(
        _ret_kernel,
        grid=(b, t // L),
        in_specs=[
            pl.BlockSpec((1, L, pw), lambda bi, ci: (bi, ci, 0)),
            pl.BlockSpec((1, L, d), lambda bi, ci: (bi, ci, 0)),
            pl.BlockSpec((L, RET_DK // 2), lambda bi, ci: (ci, 0)),
            pl.BlockSpec((L, RET_DK // 2), lambda bi, ci: (ci, 0)),
            pl.BlockSpec((RET_V_DIM, d), lambda bi, ci: (0, 0)),
        ],
        out_specs=pl.BlockSpec((1, L, d), lambda bi, ci: (bi, ci, 0)),
        out_shape=jax.ShapeDtypeStruct((b, t, d), F32),
        scratch_shapes=[
            pltpu.VMEM((RET_HEADS, RET_DK, RET_DV), F32),
            pltpu.VMEM((L, RET_V_DIM), BF16),
        ],
        compiler_params=_cparams(("parallel", "arbitrary")),
        name="ret_core",
    )(proj3, x3, cos, sin, w_out_bf16)


RWKV_HALO = V7X_F32_SUBLANES


def _rwkv_proj_kernel(x_ref, xh_ref, nw_ref, mix_ref, wr_ref, wk_ref, wv_ref, w0_ref, w1_ref, w2_ref,
                      a0_ref, a1_ref, a2_ref, g1_ref, g2_ref,
                      r_ref, w_ref, k_ref, v_ref, a_ref, g_ref, hs_ref, *, tm, tiles_per_seq):
    i = pl.program_id(0)
    h = _rms(x_ref[...], nw_ref[...])
    keep = jnp.where(i % tiles_per_seq == 0, 0.0, 1.0)
    hs_ref[pl.ds(0, RWKV_HALO), :] = _rms(xh_ref[...], nw_ref[...]) * keep
    hs_ref[pl.ds(RWKV_HALO, tm), :] = h
    xx = hs_ref[pl.ds(RWKV_HALO - 1, tm), :] - h

    def mixed(idx):
        return (h + xx * mix_ref[idx:idx + 1, :]).astype(BF16)

    r_ref[...] = _dot(mixed(0), wr_ref[...]).astype(r_ref.dtype)
    lw = w0_ref[...] + _dot(jnp.tanh(_dot(mixed(1), w1_ref[...])).astype(BF16), w2_ref[...])
    w_raw = -_softplus(-lw) - 0.5
    w_ref[...] = jnp.exp(-jnp.exp(w_raw))
    k_ref[...] = _dot(mixed(2), wk_ref[...]).astype(k_ref.dtype)
    v_ref[...] = _dot(mixed(3), wv_ref[...]).astype(v_ref.dtype)
    la = a0_ref[...] + _dot(_dot(mixed(4), a1_ref[...]).astype(BF16), a2_ref[...])
    a_ref[...] = _sigmoid(la).astype(a_ref.dtype)
    g_ref[...] = _dot(_sigmoid(_dot(mixed(5), g1_ref[...])).astype(BF16), g2_ref[...]).astype(g_ref.dtype)


def _pad_cols(w, n):
    return jnp.pad(w, ((0, 0), (0, n - w.shape[1])))


def _pad_rows(w, n):
    return jnp.pad(w, ((0, n - w.shape[0]), (0, 0)))


def rwkv_proj(x2, seq_len, norm_w, mix, w_rkv, w0, w1, w2, a0, a1, a2, g1, g2):
    m, d = x2.shape
    tm = min(256, seq_len)
    tiles_per_seq = seq_len // tm
    halo_blocks = tm // RWKV_HALO
    lw = V7X_LANES
    lg = 2 * V7X_LANES
    wr, wk, wv = (w_rkv[n].astype(BF16) for n in range(3))
    w1p, w2p = _pad_cols(w1, lw).astype(BF16), _pad_rows(w2, lw).astype(BF16)
    a1p, a2p = _pad_cols(a1, lw).astype(BF16), _pad_rows(a2, lw).astype(BF16)
    g1p, g2p = _pad_cols(g1, lg).astype(BF16), _pad_rows(g2, lg).astype(BF16)
    row = lambda i: (i, 0)
    const = lambda i: (0, 0)
    out_sd = lambda dt: jax.ShapeDtypeStruct((m, d), dt)
    kern = functools.partial(_rwkv_proj_kernel, tm=tm, tiles_per_seq=tiles_per_seq)
    return pl.pallas_call(
        kern,
        grid=(m // tm,),
        in_specs=[
            pl.BlockSpec((tm, d), row),
            pl.BlockSpec((RWKV_HALO, d), lambda i: (jnp.maximum(i * halo_blocks - 1, 0), 0)),
            pl.BlockSpec((1, d), const),
            pl.BlockSpec((6, d), const),
            pl.BlockSpec((d, d), const), pl.BlockSpec((d, d), const), pl.BlockSpec((d, d), const),
            pl.BlockSpec((1, d), const), pl.BlockSpec((d, lw), const), pl.BlockSpec((lw, d), const),
            pl.BlockSpec((1, d), const), pl.BlockSpec((d, lw), const), pl.BlockSpec((lw, d), const),
            pl.BlockSpec((d, lg), const), pl.BlockSpec((lg, d), const),
        ],
        out_specs=[pl.BlockSpec((tm, d), row)] * 6,
        out_shape=[out_sd(F32), out_sd(F32), out_sd(F32), out_sd(F32), out_sd(F32), out_sd(BF16)],
        scratch_shapes=[pltpu.VMEM((tm + RWKV_HALO, d), F32)],
        compiler_params=_cparams(("parallel",)),
        name="rwkv_proj",
    )(x2, x2, norm_w.reshape(1, d), mix, wr, wk, wv, w0.reshape(1, d), w1p, w2p,
      a0.reshape(1, d), a1p, a2p, g1p, g2p)


def _rwkv_scan_kernel(r_ref, w_ref, k_ref, v_ref, a_ref, kk_ref, ka_ref, rk_ref, lnw_ref, lnb_ref,
                      y_ref, s_ref, av_ref, bv_ref, kf_ref, *, tb):
    n = RWKV_HEAD_DIM

    @pl.when(pl.program_id(0) == 0)
    def _():
        s_ref[...] = jnp.zeros_like(s_ref)

    def step(t, carry):
        r = r_ref[t]
        k = k_ref[t]
        v = v_ref[t]
        a = a_ref[t]
        kk = k * kk_ref[...]
        norm = jnp.sqrt(jnp.sum(kk * kk, axis=0, keepdims=True))
        kk = kk / jnp.maximum(norm, 1e-12)
        kf = k * (1.0 + (a - 1.0) * ka_ref[...])
        av_ref[...] = -kk
        bv_ref[...] = kk * a
        kf_ref[...] = kf
        sa = jnp.zeros((n, V7X_LANES), F32)
        for j in range(n):
            sa = sa + s_ref[j] * av_ref[j:j + 1, :]
        y = jnp.zeros((n, V7X_LANES), F32)
        for j in range(n):
            s_new = s_ref[j] * w_ref[t, j:j + 1, :] + sa * bv_ref[j:j + 1, :] + v * kf_ref[j:j + 1, :]
            s_ref[j] = s_new
            y = y + s_new * r_ref[t, j:j + 1, :]
        mu = jnp.mean(y, axis=0, keepdims=True)
        yc = y - mu
        var = jnp.mean(yc * yc, axis=0, keepdims=True)
        yn = yc * lax.rsqrt(var + RWKV_GN_EPS) * lnw_ref[...] + lnb_ref[...]
        bonus = jnp.sum(r * kf * rk_ref[...], axis=0, keepdims=True)
        y_ref[t] = yn + bonus * v
        return carry

    lax.fori_loop(0, tb, step, 0)


def rwkv_scan(r_t, w_t, k_t, v_t, a_t, kk_t, ka_t, rk_t, lnw_t, lnb_t):
    t, n, lanes = r_t.shape
    tb = min(64, t)
    blk = pl.BlockSpec((tb, n, lanes), lambda i: (i, 0, 0))
    par = pl.BlockSpec((n, lanes), lambda i: (0, 0))
    kern = functools.partial(_rwkv_scan_kernel, tb=tb)
    return pl.pallas_call(
        kern,
        grid=(t // tb,),
        in_specs=[blk] * 5 + [par] * 5,
        out_specs=blk,
        out_shape=jax.ShapeDtypeStruct((t, n, lanes), F32),
        scratch_shapes=[
            pltpu.VMEM((n, n, lanes), F32),
            pltpu.VMEM((n, lanes), F32),
            pltpu.VMEM((n, lanes), F32),
            pltpu.VMEM((n, lanes), F32),
        ],
        compiler_params=_cparams(("arbitrary",)),
        name="rwkv_scan",
    )(r_t, w_t, k_t, v_t, a_t, kk_t, ka_t, rk_t, lnw_t, lnb_t)


def _ssd_layer(x3, norm_w, w_in, conv_w, conv_b, dt_bias, a_log, d_skip, ssd_norm_w, w_out):
    b, t, d = x3.shape
    x2 = x3.reshape(b * t, d)
    main = SSD_D_INNER + SSD_CONV_DIM
    proj = norm_matmul(x2, norm_w, w_in[:, :main].astype(BF16), BF16, 512)
    w_dt = _pad_cols(w_in[:, main:], V7X_LANES).astype(BF16)
    dt = norm_matmul(x2, norm_w, w_dt, F32, V7X_LANES)
    return ssd_core(proj.reshape(b, t, main), dt.reshape(b, t, V7X_LANES), x3, conv_w, conv_b,
                    dt_bias, a_log, d_skip, ssd_norm_w, w_out.astype(BF16))


def _to_lanes(z2, b, t):
    z = z2.reshape(b, t, RWKV_HEADS, RWKV_HEAD_DIM)
    return jnp.transpose(z, (1, 3, 0, 2)).reshape(t, RWKV_HEAD_DIM, b * RWKV_HEADS)


def _param_to_lanes(p, b):
    return jnp.tile(p.reshape(RWKV_HEADS, RWKV_HEAD_DIM).T, (1, b))


def _rwkv_layer(x3, norm_w, mix, w_rkv, w0, w1, w2, a0, a1, a2, g1, g2, k_k, k_a, r_k, ln_w, ln_b, w_out):
    b, t, d = x3.shape
    x2 = x3.reshape(b * t, d)
    r, w, k, v, a, g = rwkv_proj(x2, t, norm_w, mix, w_rkv, w0, w1, w2, a0, a1, a2, g1, g2)
    y_t = rwkv_scan(*(_to_lanes(z, b, t) for z in (r, w, k, v, a)),
                    *(_param_to_lanes(p, b) for p in (k_k, k_a, r_k, ln_w, ln_b)))
    y = jnp.transpose(y_t.reshape(t, RWKV_HEAD_DIM, b, RWKV_HEADS), (2, 0, 3, 1)).reshape(b * t, d)
    return gate_matmul_res(x2, y, g, w_out.astype(BF16)).reshape(b, t, d)


def _gla_layer(x3, norm_w, w_in, w_gk2, b_gk2, gla_norm_w, w_out):
    b, t, d = x3.shape
    n_pad = 2 * GLA_KEY_DIM + 2 * GLA_VALUE_DIM + V7X_LANES
    proj = norm_matmul(x3.reshape(b * t, d), norm_w, _pad_cols(w_in, n_pad).astype(BF16), BF16, 640)
    return gla_core(proj.reshape(b, t, n_pad), x3, w_gk2, b_gk2, gla_norm_w, w_out.astype(BF16))


def _deinterleave_heads(w_cols, heads, dk):
    dm = w_cols.shape[0]
    w4 = w_cols.reshape(dm, heads, dk // 2, 2)
    return jnp.transpose(w4, (0, 1, 3, 2)).reshape(dm, heads * dk)


def _ret_layer(x3, norm_w, w_in, w_out):
    b, t, d = x3.shape
    wq = _deinterleave_heads(w_in[:, :RET_QK_DIM], RET_HEADS, RET_DK)
    wk = _deinterleave_heads(w_in[:, RET_QK_DIM:2 * RET_QK_DIM], RET_HEADS, RET_DK)
    w_perm = jnp.concatenate([wq, wk, w_in[:, 2 * RET_QK_DIM:]], axis=1).astype(BF16)
    proj = norm_matmul(x3.reshape(b * t, d), norm_w, w_perm, BF16, 512)
    inv = 1.0 / (ROPE_BASE ** jnp.linspace(0.0, 1.0, RET_DK // 2, dtype=F32))
    ang = jnp.arange(t, dtype=F32)[:, None] * inv[None]
    return ret_core(proj.reshape(b, t, w_perm.shape[1]), x3, jnp.cos(ang), jnp.sin(ang), w_out.astype(BF16))


def kernel(x, norm_mix, norm_ffn, norm_final, ssd_w_in, ssd_conv_w, ssd_conv_b, ssd_dt_bias, ssd_a_log, ssd_d, ssd_norm_w, ssd_w_out, rwkv_mix, rwkv_w_rkv, rwkv_w0, rwkv_w1, rwkv_w2, rwkv_a0, rwkv_a1, rwkv_a2, rwkv_g1, rwkv_g2, rwkv_k_k, rwkv_k_a, rwkv_r_k, rwkv_ln_w, rwkv_ln_b, rwkv_w_out, gla_w_in, gla_w_gk2, gla_b_gk2, gla_norm_w, gla_w_out, ret_w_in, ret_w_out, ffn_w_up, ffn_conv_w, ffn_conv_b, ffn_w_down):
    b, t, d = x.shape
    depth = norm_mix.shape[0]
    for i in range(depth):
        m, j = i % 4, i // 4
        if m == 0:
            x = _ssd_layer(x, norm_mix[i], ssd_w_in[j], ssd_conv_w[j], ssd_conv_b[j], ssd_dt_bias[j],
                           ssd_a_log[j], ssd_d[j], ssd_norm_w[j], ssd_w_out[j])
        elif m == 1:
            x = _rwkv_layer(x, norm_mix[i], rwkv_mix[j], rwkv_w_rkv[j], rwkv_w0[j], rwkv_w1[j], rwkv_w2[j],
                            rwkv_a0[j], rwkv_a1[j], rwkv_a2[j], rwkv_g1[j], rwkv_g2[j], rwkv_k_k[j],
                            rwkv_k_a[j], rwkv_r_k[j], rwkv_ln_w[j], rwkv_ln_b[j], rwkv_w_out[j])
        elif m == 2:
            x = _gla_layer(x, norm_mix[i], gla_w_in[j], gla_w_gk2[j], gla_b_gk2[j], gla_norm_w[j], gla_w_out[j])
        else:
            x = _ret_layer(x, norm_mix[i], ret_w_in[j], ret_w_out[j])
        x = ffn(x.reshape(b * t, d), t, norm_ffn[i], ffn_w_up[i].astype(BF16), ffn_conv_w[i], ffn_conv_b[i],
                ffn_w_down[i].astype(BF16), norm_final, final_norm=(i == depth - 1)).reshape(b, t, d)
    return x
```

```python
import functools
import math

import jax
import jax.numpy as jnp
from jax import lax
from jax.experimental import pallas as pl
from jax.experimental.pallas import tpu as pltpu

D_MODEL = 1024
NORM_EPS = 1e-5
CHUNK = 128

SSD_D_INNER = 2048
SSD_HEAD_DIM = 64
SSD_HEADS = 32
SSD_GROUPS = 8
SSD_STATE = 128
SSD_CONV = 4
SSD_CONV_DIM = SSD_D_INNER + 2 * SSD_GROUPS * SSD_STATE
SSD_HEADS_PER_GROUP = SSD_HEADS // SSD_GROUPS
SSD_GROUP_V = SSD_HEADS_PER_GROUP * SSD_HEAD_DIM

RWKV_HEADS = 16
RWKV_HEAD_DIM = 64
RWKV_GN_EPS = 64e-5

GLA_HEADS = 4
GLA_DK = 128
GLA_DV = 256
GLA_KEY_DIM = 512
GLA_VALUE_DIM = 1024
GLA_GATE_RANK = 16
GLA_GATE_NORM = 16.0
GLA_SUB = 16

RET_HEADS = 4
RET_DK = 256
RET_DV = 512
RET_QK_DIM = 1024
RET_V_DIM = 2048
ROPE_BASE = 10000.0

FFN_HIDDEN = 2816
FFN_CONV = 3

V7X_LANES = 128
V7X_BF16_SUBLANES = 16
V7X_F32_SUBLANES = 8
V7X_VMEM_LIMIT_BYTES = 56 * 1024 * 1024

F32 = jnp.float32
BF16 = jnp.bfloat16
NEG_BIG = -1e30


def _cparams(sem):
    return pltpu.CompilerParams(dimension_semantics=sem, vmem_limit_bytes=V7X_VMEM_LIMIT_BYTES)


def _rms(x, w):
    return x * lax.rsqrt(jnp.mean(x * x, axis=-1, keepdims=True) + NORM_EPS) * w


def _sigmoid(x):
    return 1.0 / (1.0 + jnp.exp(-x))


def _silu(x):
    return x * _sigmoid(x)


def _softplus(x):
    return jnp.maximum(x, 0.0) + jnp.log(1.0 + jnp.exp(-jnp.abs(x)))


def _dot(a, b):
    return jnp.dot(a, b, preferred_element_type=F32)


def _dot_nt(a, b):
    return lax.dot_general(a, b, (((1,), (1,)), ((), ())), preferred_element_type=F32)


def _dot_tn(a, b):
    return lax.dot_general(a, b, (((0,), (0,)), ((), ())), preferred_element_type=F32)


def _split_dot(t_bf16, x_f32):
    hi = x_f32.astype(BF16)
    lo = (x_f32 - hi.astype(F32)).astype(BF16)
    return _dot(t_bf16, hi) + _dot(t_bf16, lo)


def _tril_ones(n, dtype):
    r = lax.broadcasted_iota(jnp.int32, (n, n), 0)
    c = lax.broadcasted_iota(jnp.int32, (n, n), 1)
    return jnp.where(r >= c, 1.0, 0.0).astype(dtype)


def _norm_matmul_kernel(x_ref, nw_ref, w_ref, o_ref, xn_ref):
    @pl.when(pl.program_id(1) == 0)
    def _():
        xn_ref[...] = _rms(x_ref[...], nw_ref[...]).astype(BF16)

    o_ref[...] = _dot(xn_ref[...], w_ref[...]).astype(o_ref.dtype)


def norm_matmul(x2, norm_w, w_bf16, out_dtype, tn):
    m, d = x2.shape
    n = w_bf16.shape[1]
    tm = min(1024, m)
    return pl.pallas_call(
        _norm_matmul_kernel,
        grid=(m // tm, n // tn),
        in_specs=[
            pl.BlockSpec((tm, d), lambda i, j: (i, 0)),
            pl.BlockSpec((1, d), lambda i, j: (0, 0)),
            pl.BlockSpec((d, tn), lambda i, j: (0, j)),
        ],
        out_specs=pl.BlockSpec((tm, tn), lambda i, j: (i, j)),
        out_shape=jax.ShapeDtypeStruct((m, n), out_dtype),
        scratch_shapes=[pltpu.VMEM((tm, d), BF16)],
        compiler_params=_cparams(("parallel", "arbitrary")),
        name="norm_matmul",
    )(x2, norm_w.reshape(1, d), w_bf16)


def _gate_matmul_res_kernel(x_ref, y_ref, g_ref, w_ref, o_ref):
    yg = (y_ref[...].astype(F32) * g_ref[...].astype(F32)).astype(BF16)
    o_ref[...] = x_ref[...] + _dot(yg, w_ref[...])


def gate_matmul_res(x2, y2, g2, w_bf16):
    m, d = x2.shape
    k = y2.shape[1]
    tm = min(1024, m)
    return pl.pallas_call(
        _gate_matmul_res_kernel,
        grid=(m // tm,),
        in_specs=[
            pl.BlockSpec((tm, d), lambda i: (i, 0)),
            pl.BlockSpec((tm, k), lambda i: (i, 0)),
            pl.BlockSpec((tm, k), lambda i: (i, 0)),
            pl.BlockSpec((k, d), lambda i: (0, 0)),
        ],
        out_specs=pl.BlockSpec((tm, d), lambda i: (i, 0)),
        out_shape=jax.ShapeDtypeStruct((m, d), F32),
        compiler_params=_cparams(("parallel",)),
        name="gate_matmul_res",
    )(x2, y2, g2, w_bf16)


FFN_HALO = V7X_BF16_SUBLANES


def _ffn_kernel(x_ref, xh_ref, nw_ref, wv_ref, wg_ref, cwv_ref, cwg_ref, cbv_ref, cbg_ref,
                wd_ref, fnw_ref, o_ref, xn_ref, hv_ref, hg_ref, act_ref, *, tm, tf, sub, tiles_per_seq,
                final_norm):
    i = pl.program_id(0)
    j = pl.program_id(1)

    @pl.when(j == 0)
    def _():
        x = x_ref[...]
        xn_ref[pl.ds(FFN_HALO, tm), :] = _rms(x, nw_ref[...]).astype(BF16)
        keep = jnp.where(i % tiles_per_seq == 0, 0.0, 1.0)
        xn_ref[pl.ds(0, FFN_HALO), :] = (_rms(xh_ref[...], nw_ref[...]) * keep).astype(BF16)
        o_ref[...] = x

    def conv(h_ref, cw_ref, cb_ref, cs, w):
        return (cw_ref[2:3, cs] * h_ref[pl.ds(FFN_HALO, tm), pl.ds(0, w)]
                + cw_ref[1:2, cs] * h_ref[pl.ds(FFN_HALO - 1, tm), pl.ds(0, w)]
                + cw_ref[0:1, cs] * h_ref[pl.ds(FFN_HALO - 2, tm), pl.ds(0, w)]
                + cb_ref[:, cs])

    for c0 in range(0, tf, sub):
        w = min(sub, tf - c0)
        cs = pl.ds(c0, w)
        hv_ref[:, pl.ds(0, w)] = _dot(xn_ref[...], wv_ref[:, cs])
        hg_ref[:, pl.ds(0, w)] = _dot(xn_ref[...], wg_ref[:, cs])
        val = conv(hv_ref, cwv_ref, cbv_ref, cs, w)
        gate = conv(hg_ref, cwg_ref, cbg_ref, cs, w)
        act_ref[:, cs] = (val * _silu(gate)).astype(BF16)

    o_ref[...] += _dot(act_ref[...], wd_ref[...])

    if final_norm:
        @pl.when(j == pl.num_programs(1) - 1)
        def _():
            o_ref[...] = _rms(o_ref[...], fnw_ref[...])


def ffn(x2, seq_len, norm_w, w_up_bf16, conv_w, conv_b, w_down_bf16, final_norm_w, final_norm):
    m, d = x2.shape
    f = FFN_HIDDEN
    tm = min(1024, seq_len)
    nf = 2
    tf = f // nf
    sub = 2 * V7X_LANES
    tiles_per_seq = seq_len // tm
    halo_blocks = tm // FFN_HALO
    kern = functools.partial(_ffn_kernel, tm=tm, tf=tf, sub=sub, tiles_per_seq=tiles_per_seq,
                             final_norm=final_norm)
    return pl.pallas_call(
        kern,
        grid=(m // tm, nf),
        in_specs=[
            pl.BlockSpec((tm, d), lambda i, j: (i, 0)),
            pl.BlockSpec((FFN_HALO, d), lambda i, j: (jnp.maximum(i * halo_blocks - 1, 0), 0)),
            pl.BlockSpec((1, d), lambda i, j: (0, 0)),
            pl.BlockSpec((d, tf), lambda i, j: (0, j)),
            pl.BlockSpec((d, tf), lambda i, j: (0, nf + j)),
            pl.BlockSpec((FFN_CONV, tf), lambda i, j: (0, j)),
            pl.BlockSpec((FFN_CONV, tf), lambda i, j: (0, nf + j)),
            pl.BlockSpec((1, tf), lambda i, j: (0, j)),
            pl.BlockSpec((1, tf), lambda i, j: (0, nf + j)),
            pl.BlockSpec((tf, d), lambda i, j: (j, 0)),
            pl.BlockSpec((1, d), lambda i, j: (0, 0)),
        ],
        out_specs=pl.BlockSpec((tm, d), lambda i, j: (i, 0)),
        out_shape=jax.ShapeDtypeStruct((m, d), F32),
        scratch_shapes=[
            pltpu.VMEM((tm + FFN_HALO, d), BF16),
            pltpu.VMEM((tm + FFN_HALO, sub), F32),
            pltpu.VMEM((tm + FFN_HALO, sub), F32),
            pltpu.VMEM((tm, tf), BF16),
        ],
        compiler_params=_cparams(("parallel", "arbitrary")),
        name="ffn",
    )(x2, x2, norm_w.reshape(1, d), w_up_bf16, w_up_bf16, conv_w, conv_w,
      conv_b.reshape(1, 2 * f), conv_b.reshape(1, 2 * f), w_down_bf16, final_norm_w.reshape(1, d))


SSD_HALO = V7X_BF16_SUBLANES


def _ssd_kernel(proj_ref, dt_ref, x_ref, cw_ref, cb_ref, dtb_ref, alog_ref, dexp_ref, nw_ref,
                wout_ref, e_ref, o_ref, s_ref, halo_ref, ext_ref, act_ref, yn_ref):
    L = CHUNK
    c = pl.program_id(1)

    @pl.when(c == 0)
    def _():
        s_ref[...] = jnp.zeros_like(s_ref)
        halo_ref[...] = jnp.zeros_like(halo_ref)

    ext_ref[pl.ds(0, SSD_HALO), :] = halo_ref[...]
    ext_ref[pl.ds(SSD_HALO, L), :] = proj_ref[0, :, pl.ds(SSD_D_INNER, SSD_CONV_DIM)]
    halo_ref[...] = ext_ref[pl.ds(L, SSD_HALO), :]
    sel_r = lax.broadcasted_iota(jnp.int32, (SSD_CONV * L, L + SSD_HALO), 0)
    sel_c = lax.broadcasted_iota(jnp.int32, (SSD_CONV * L, L + SSD_HALO), 1)
    src = jnp.bitwise_and(sel_r, L - 1) + jnp.right_shift(sel_r, L.bit_length() - 1) + (SSD_HALO - SSD_CONV + 1)
    sel = jnp.where(sel_c == src, 1.0, 0.0).astype(BF16)
    panel = 512
    for p in range(SSD_CONV_DIM // panel):
        cs = pl.ds(p * panel, panel)
        shifted = _dot(sel, ext_ref[:, cs])
        acc = cb_ref[:, cs] + cw_ref[0:1, cs] * shifted[0:L]
        for tap in range(1, SSD_CONV):
            acc = acc + cw_ref[tap:tap + 1, cs] * shifted[tap * L:(tap + 1) * L]
        act_ref[:, cs] = _silu(acc)

    dt = _softplus(dt_ref[0] + dtb_ref[...])
    a = -jnp.exp(alog_ref[...])
    tril_b = _tril_ones(L, BF16)
    cum = _split_dot(tril_b, dt * a)
    cum_last = cum[L - 1:L, :]
    cum_t = cum.T
    dt_t = dt.T
    e_in = jnp.exp(cum)
    c_st = dt * jnp.exp(cum_last - cum)
    g_last = jnp.broadcast_to(jnp.exp(cum_last), (V7X_F32_SUBLANES, V7X_LANES))
    stacked = jnp.concatenate([e_in, c_st, g_last], axis=0).astype(BF16)
    expanded = _dot(stacked, e_ref[...])
    e_exp = expanded[0:L]
    c_exp = expanded[L:2 * L]
    g_exp = expanded[2 * L:2 * L + 1]

    row = lax.broadcasted_iota(jnp.int32, (L, L), 0)
    col = lax.broadcasted_iota(jnp.int32, (L, L), 1)
    causal = row >= col
    vcol = lax.broadcasted_iota(jnp.int32, (L, SSD_GROUP_V), 1) // SSD_HEAD_DIM

    for g in range(SSD_GROUPS):
        vs = pl.ds(g * SSD_GROUP_V, SSD_GROUP_V)
        kq = act_ref[:, pl.ds(SSD_D_INNER + g * SSD_STATE, SSD_STATE)].astype(BF16)
        qq = act_ref[:, pl.ds(SSD_D_INNER + (SSD_GROUPS + g) * SSD_STATE, SSD_STATE)].astype(BF16)
        xs = act_ref[:, vs]
        sb = _dot_nt(qq, kq)
        s_old = s_ref[g]
        q_s = _dot(qq, s_old.astype(BF16))
        probs = []
        vblk = []
        for hh in range(SSD_HEADS_PER_GROUP):
            h = g * SSD_HEADS_PER_GROUP + hh
            seg = cum[:, h:h + 1] - cum_t[h:h + 1, :]
            dec = jnp.exp(jnp.where(causal, seg, NEG_BIG)) * dt_t[h:h + 1, :]
            probs.append((sb * dec).astype(BF16))
            vblk.append(jnp.where(vcol == hh, xs, 0.0).astype(BF16))
        p_cat = jnp.concatenate(probs, axis=1)
        v_blk = jnp.concatenate(vblk, axis=0)
        y = _dot(p_cat, v_blk) + e_exp[:, g * SSD_GROUP_V:(g + 1) * SSD_GROUP_V] * q_s
        xs_scaled = (xs * c_exp[:, g * SSD_GROUP_V:(g + 1) * SSD_GROUP_V]).astype(BF16)
        s_ref[g] = s_old * g_exp[:, g * SSD_GROUP_V:(g + 1) * SSD_GROUP_V] + _dot_tn(kq, xs_scaled)
        y = y + dexp_ref[:, vs] * xs
        y = y * _silu(proj_ref[0, :, vs].astype(F32))
        y = y * lax.rsqrt(jnp.mean(y * y, axis=-1, keepdims=True) + NORM_EPS)
        yn_ref[:, vs] = (y * nw_ref[:, vs]).astype(BF16)

    o_ref[0] = x_ref[0] + _dot(yn_ref[...], wout_ref[...])


def ssd_core(proj3, dt3, x3, conv_w, conv_b, dt_bias, a_log, d_skip, norm_w, w_out_bf16):
    b, t, d = x3.shape
    L = CHUNK
    pad = V7X_LANES - SSD_HEADS
    dtb = jnp.pad(dt_bias, (0, pad)).reshape(1, V7X_LANES)
    alog = jnp.pad(a_log, (0, pad)).reshape(1, V7X_LANES)
    dexp = jnp.repeat(d_skip, SSD_HEAD_DIM).reshape(1, SSD_D_INNER)
    expand = (jnp.arange(V7X_LANES)[:, None] == (jnp.arange(SSD_D_INNER)[None, :] // SSD_HEAD_DIM)).astype(BF16)
    pw = proj3.shape[2]
    const = lambda bi, ci: (0, 0)
    return pl.pallas_call(
        _ssd_kernel,
        grid=(b, t // L),
        in_specs=[
            pl.BlockSpec((1, L, pw), lambda bi, ci: (bi, ci, 0)),
            pl.BlockSpec((1, L, V7X_LANES), lambda bi, ci: (bi, ci, 0)),
            pl.BlockSpec((1, L, d), lambda bi, ci: (bi, ci, 0)),
            pl.BlockSpec((SSD_CONV, SSD_CONV_DIM), const),
            pl.BlockSpec((1, SSD_CONV_DIM), const),
            pl.BlockSpec((1, V7X_LANES), const),
            pl.BlockSpec((1, V7X_LANES), const),
            pl.BlockSpec((1, SSD_D_INNER), const),
            pl.BlockSpec((1, SSD_D_INNER), const),
            pl.BlockSpec((SSD_D_INNER, d), const),
            pl.BlockSpec((V7X_LANES, SSD_D_INNER), const),
        ],
        out_specs=pl.BlockSpec((1, L, d), lambda bi, ci: (bi, ci, 0)),
        out_shape=jax.ShapeDtypeStruct((b, t, d), F32),
        scratch_shapes=[
            pltpu.VMEM((SSD_GROUPS, SSD_STATE, SSD_GROUP_V), F32),
            pltpu.VMEM((SSD_HALO, SSD_CONV_DIM), BF16),
            pltpu.VMEM((L + SSD_HALO, SSD_CONV_DIM), BF16),
            pltpu.VMEM((L, SSD_CONV_DIM), F32),
            pltpu.VMEM((L, SSD_D_INNER), BF16),
        ],
        compiler_params=_cparams(("parallel", "arbitrary")),
        name="ssd_core",
    )(proj3, dt3, x3, conv_w, conv_b.reshape(1, SSD_CONV_DIM), dtb, alog, dexp,
      norm_w.reshape(1, SSD_D_INNER), w_out_bf16, expand)


def _gla_kernel(proj_ref, x_ref, wgk_ref, bgk_ref, nw_ref, wout_ref, o_ref, s_ref, yn_ref):
    L = CHUNK
    c = pl.program_id(1)

    @pl.when(c == 0)
    def _():
        s_ref[...] = jnp.zeros_like(s_ref)

    gk_off = 2 * GLA_KEY_DIM + 2 * GLA_VALUE_DIM
    u = _dot(proj_ref[0, :, pl.ds(gk_off, V7X_LANES)], wgk_ref[...]) + bgk_ref[...]
    log_a = (jnp.minimum(u, 0.0) - jnp.log(1.0 + jnp.exp(-jnp.abs(u)))) * (1.0 / GLA_GATE_NORM)
    gcum = _split_dot(_tril_ones(L, BF16), log_a)

    row = lax.broadcasted_iota(jnp.int32, (L, L), 0)
    col = lax.broadcasted_iota(jnp.int32, (L, L), 1)
    causal = row >= col
    krow = lax.broadcasted_iota(jnp.int32, (L, GLA_DK), 0)
    nsub = L // GLA_SUB

    for h in range(GLA_HEADS):
        ks = pl.ds(h * GLA_DK, GLA_DK)
        q = proj_ref[0, :, ks].astype(F32) * (GLA_DK ** -0.5)
        k = proj_ref[0, :, pl.ds(GLA_KEY_DIM + h * GLA_DK, GLA_DK)].astype(F32)
        v = proj_ref[0, :, pl.ds(2 * GLA_KEY_DIM + h * GLA_DV, GLA_DV)]
        gh = gcum[:, h * GLA_DK:(h + 1) * GLA_DK]
        blocks = []
        for i in range(nsub):
            r0 = i * GLA_SUB
            if i == 0:
                g_ref_row = jnp.zeros((1, GLA_DK), F32)
            else:
                g_ref_row = gh[r0 - 1:r0, :]
            q_i = (q[r0:r0 + GLA_SUB] * jnp.exp(gh[r0:r0 + GLA_SUB] - g_ref_row)).astype(BF16)
            k_i = (k * jnp.exp(jnp.where(krow < r0 + GLA_SUB, g_ref_row - gh, NEG_BIG))).astype(BF16)
            blocks.append(_dot_nt(q_i, k_i))
        scores = jnp.where(causal, jnp.concatenate(blocks, axis=0), 0.0).astype(BF16)
        st_old = s_ref[h]
        o = _dot(scores, v) + _dot_nt((q * jnp.exp(gh)).astype(BF16), st_old.astype(BF16))
        g_last = gh[L - 1:L, :]
        k_hat = (k * jnp.exp(g_last - gh)).astype(BF16)
        s_ref[h] = st_old * jnp.exp(g_last) + _dot_tn(v, k_hat)
        o = o * lax.rsqrt(jnp.mean(o * o, axis=-1, keepdims=True) + NORM_EPS) * nw_ref[...]
        gate = proj_ref[0, :, pl.ds(2 * GLA_KEY_DIM + GLA_VALUE_DIM + h * GLA_DV, GLA_DV)].astype(F32)
        yn_ref[:, pl.ds(h * GLA_DV, GLA_DV)] = (o * _silu(gate)).astype(BF16)

    o_ref[0] = x_ref[0] + _dot(yn_ref[...], wout_ref[...])


def gla_core(proj3, x3, w_gk2, b_gk2, norm_w, w_out_bf16):
    b, t, d = x3.shape
    L = CHUNK
    pw = proj3.shape[2]
    wgk = jnp.pad(w_gk2, ((0, V7X_LANES - GLA_GATE_RANK), (0, 0))).astype(BF16)
    const = lambda bi, ci: (0, 0)
    return pl.pallas_call(
        _gla_kernel,
        grid=(b, t // L),
        in_specs=[
            pl.BlockSpec((1, L, pw), lambda bi, ci: (bi, ci, 0)),
            pl.BlockSpec((1, L, d), lambda bi, ci: (bi, ci, 0)),
            pl.BlockSpec((V7X_LANES, GLA_KEY_DIM), const),
            pl.BlockSpec((1, GLA_KEY_DIM), const),
            pl.BlockSpec((1, GLA_DV), const),
            pl.BlockSpec((GLA_VALUE_DIM, d), const),
        ],
        out_specs=pl.BlockSpec((1, L, d), lambda bi, ci: (bi, ci, 0)),
        out_shape=jax.ShapeDtypeStruct((b, t, d), F32),
        scratch_shapes=[
            pltpu.VMEM((GLA_HEADS, GLA_DV, GLA_DK), F32),
            pltpu.VMEM((L, GLA_VALUE_DIM), BF16),
        ],
        compiler_params=_cparams(("parallel", "arbitrary")),
        name="gla_core",
    )(proj3, x3, wgk, b_gk2.reshape(1, GLA_KEY_DIM), norm_w.reshape(1, GLA_DV), w_out_bf16)


def _ret_kernel(proj_ref, x_ref, cos_ref, sin_ref, wout_ref, o_ref, s_ref, yn_ref):
    L = CHUNK
    c = pl.program_id(1)

    @pl.when(c == 0)
    def _():
        s_ref[...] = jnp.zeros_like(s_ref)

    cos = cos_ref[...]
    sin = sin_ref[...]
    half = RET_DK // 2

    def rotary(xh):
        xe = xh[:, :half]
        xo = xh[:, half:]
        return jnp.concatenate([xe * cos - xo * sin, xo * cos + xe * sin], axis=1)

    rowf = lax.broadcasted_iota(jnp.int32, (L, L), 0).astype(F32)
    colf = lax.broadcasted_iota(jnp.int32, (L, L), 1).astype(F32)
    diff = rowf - colf
    trow = lax.broadcasted_iota(jnp.int32, (L, V7X_LANES), 0).astype(F32)

    for h in range(RET_HEADS):
        lg = math.log(1.0 - 2.0 ** (-5.0 - h))
        q = rotary(proj_ref[0, :, pl.ds(h * RET_DK, RET_DK)].astype(F32))
        k = rotary(proj_ref[0, :, pl.ds(RET_QK_DIM + h * RET_DK, RET_DK)].astype(F32)) * (RET_DK ** -0.5)
        v = proj_ref[0, :, pl.ds(2 * RET_QK_DIM + h * RET_DV, RET_DV)]
        qb = q.astype(BF16)
        dec = jnp.exp(jnp.where(diff >= 0.0, diff * lg, NEG_BIG))
        p = (_dot_nt(qb, k.astype(BF16)) * dec).astype(BF16)
        e_in = jnp.exp((trow + 1.0) * lg)
        e_in = jnp.concatenate([e_in] * (RET_DV // V7X_LANES), axis=1)
        s_old = s_ref[h]
        o = _dot(p, v) + e_in * _dot(qb, s_old.astype(BF16))
        e_out = jnp.exp((float(L - 1) - trow) * lg)
        k_hat = (k * jnp.concatenate([e_out] * (RET_DK // V7X_LANES), axis=1)).astype(BF16)
        s_ref[h] = s_old * math.exp(L * lg) + _dot_tn(k_hat, v)
        o = o * lax.rsqrt(jnp.mean(o * o, axis=-1, keepdims=True) + NORM_EPS)
        gate = proj_ref[0, :, pl.ds(2 * RET_QK_DIM + RET_V_DIM + h * RET_DV, RET_DV)].astype(F32)
        yn_ref[:, pl.ds(h * RET_DV, RET_DV)] = (_silu(gate) * o).astype(BF16)

    o_ref[0] = x_ref[0] + _dot(yn_ref[...], wout_ref[...])


def ret_core(proj3, x3, cos, sin, w_out_bf16):
    b, t, d = x3.shape
    L = CHUNK
    pw = proj3.shape[2]
    return pl.pallas_call(
        _ret_kernel,
        grid=(b, t // L),
        in_specs=[
            pl.BlockSpec((1, L, pw), lambda bi, ci: (bi, ci, 0)),
            pl.BlockSpec((1, L, d), lambda bi, ci: (bi, ci, 0)),
            pl.BlockSpec((L, RET_DK // 2), lambda bi, ci: (ci, 0)),
            pl.BlockSpec((L, RET_DK // 2), lambda bi, ci: (ci, 0)),
            pl.BlockSpec((RET_V_DIM, d), lambda bi, ci: (0, 0)),
        ],
        out_specs=pl.BlockSpec((1, L, d), lambda bi, ci: (bi, ci, 0)),
        out_shape=jax.ShapeDtypeStruct((b, t, d), F32),
        scratch_shapes=[
            pltpu.VMEM((RET_HEADS, RET_DK, RET_DV), F32),
            pltpu.VMEM((L, RET_V_DIM), BF16),
        ],
        compiler_params=_cparams(("parallel", "arbitrary")),
        name="ret_core",
    )(proj3, x3, cos, sin, w_out_bf16)


RWKV_HALO = V7X_F32_SUBLANES


def _rwkv_proj_kernel(x_ref, xh_ref, nw_ref, mix_ref, wr_ref, wk_ref, wv_ref, w0_ref, w1_ref, w2_ref,
                      a0_ref, a1_ref, a2_ref, g1_ref, g2_ref,
                      r_ref, w_ref, k_ref, v_ref, a_ref, g_ref, hs_ref, *, tm, tiles_per_seq):
    i = pl.program_id(0)
    h = _rms(x_ref[...], nw_ref[...])
    keep = jnp.where(i % tiles_per_seq == 0, 0.0, 1.0)
    hs_ref[pl.ds(0, RWKV_HALO), :] = _rms(xh_ref[...], nw_ref[...]) * keep
    hs_ref[pl.ds(RWKV_HALO, tm), :] = h
    xx = hs_ref[pl.ds(RWKV_HALO - 1, tm), :] - h

    def mixed(idx):
        return (h + xx * mix_ref[idx:idx + 1, :]).astype(BF16)

    r_ref[...] = _dot(mixed(0), wr_ref[...]).astype(r_ref.dtype)
    lw = w0_ref[...] + _dot(jnp.tanh(_dot(mixed(1), w1_ref[...])).astype(BF16), w2_ref[...])
    w_raw = -_softplus(-lw) - 0.5
    w_ref[...] = jnp.exp(-jnp.exp(w_raw))
    k_ref[...] = _dot(mixed(2), wk_ref[...]).astype(k_ref.dtype)
    v_ref[...] = _dot(mixed(3), wv_ref[...]).astype(v_ref.dtype)
    la = a0_ref[...] + _dot(_dot(mixed(4), a1_ref[...]).astype(BF16), a2_ref[...])
    a_ref[...] = _sigmoid(la).astype(a_ref.dtype)
    g_ref[...] = _dot(_sigmoid(_dot(mixed(5), g1_ref[...])).astype(BF16), g2_ref[...]).astype(g_ref.dtype)


def _pad_cols(w, n):
    return jnp.pad(w, ((0, 0), (0, n - w.shape[1])))


def _pad_rows(w, n):
    return jnp.pad(w, ((0, n - w.shape[0]), (0, 0)))


def rwkv_proj(x2, seq_len, norm_w, mix, w_rkv, w0, w1, w2, a0, a1, a2, g1, g2):
    m, d = x2.shape
    tm = min(256, seq_len)
    tiles_per_seq = seq_len // tm
    halo_blocks = tm // RWKV_HALO
    lw = V7X_LANES
    lg = 2 * V7X_LANES
    wr, wk, wv = (w_rkv[n].astype(BF16) for n in range(3))
    w1p, w2p = _pad_cols(w1, lw).astype(BF16), _pad_rows(w2, lw).astype(BF16)
    a1p, a2p = _pad_cols(a1, lw).astype(BF16), _pad_rows(a2, lw).astype(BF16)
    g1p, g2p = _pad_cols(g1, lg).astype(BF16), _pad_rows(g2, lg).astype(BF16)
    row = lambda i: (i, 0)
    const = lambda i: (0, 0)
    out_sd = lambda dt: jax.ShapeDtypeStruct((m, d), dt)
    kern = functools.partial(_rwkv_proj_kernel, tm=tm, tiles_per_seq=tiles_per_seq)
    return pl.pallas_call(
        kern,
        grid=(m // tm,),
        in_specs=[
            pl.BlockSpec((tm, d), row),
            pl.BlockSpec((RWKV_HALO, d), lambda i: (jnp.maximum(i * halo_blocks - 1, 0), 0)),
            pl.BlockSpec((1, d), const),
            pl.BlockSpec((6, d), const),
            pl.BlockSpec((d, d), const), pl.BlockSpec((d, d), const), pl.BlockSpec((d, d), const),
            pl.BlockSpec((1, d), const), pl.BlockSpec((d, lw), const), pl.BlockSpec((lw, d), const),
            pl.BlockSpec((1, d), const), pl.BlockSpec((d, lw), const), pl.BlockSpec((lw, d), const),
            pl.BlockSpec((d, lg), const), pl.BlockSpec((lg, d), const),
        ],
        out_specs=[pl.BlockSpec((tm, d), row)] * 6,
        out_shape=[out_sd(BF16), out_sd(F32), out_sd(BF16), out_sd(BF16), out_sd(BF16), out_sd(BF16)],
        scratch_shapes=[pltpu.VMEM((tm + RWKV_HALO, d), F32)],
        compiler_params=_cparams(("parallel",)),
        name="rwkv_proj",
    )(x2, x2, norm_w.reshape(1, d), mix, wr, wk, wv, w0.reshape(1, d), w1p, w2p,
      a0.reshape(1, d), a1p, a2p, g1p, g2p)


def _rwkv_scan_kernel(r_ref, w_ref, k_ref, v_ref, a_ref, kk_ref, ka_ref, rk_ref, lnw_ref, lnb_ref,
                      y_ref, s_ref, av_ref, bv_ref, kf_ref, rf_ref, yr_ref, *, tb, unroll, side_unroll):
    n = RWKV_HEAD_DIM

    @pl.when(pl.program_id(0) == 0)
    def _():
        s_ref[...] = jnp.zeros_like(s_ref)

    def prepare(t, carry):
        r = r_ref[t].astype(F32)
        k = k_ref[t].astype(F32)
        a = a_ref[t].astype(F32)
        kk = k * kk_ref[...]
        norm = jnp.sqrt(jnp.sum(kk * kk, axis=0, keepdims=True))
        kk = kk / jnp.maximum(norm, 1e-12)
        av_ref[t] = -kk
        bv_ref[t] = kk * a
        kf_ref[t] = k * (1.0 + (a - 1.0) * ka_ref[...])
        rf_ref[t] = r
        return carry

    lax.fori_loop(0, tb, prepare, 0, unroll=side_unroll)

    def step(t, carry):
        v = v_ref[t].astype(F32)

        def sa_body(j, sa):
            return sa + s_ref[j] * av_ref[t, pl.ds(j, 1), :]

        sa = lax.fori_loop(0, n, sa_body, jnp.zeros((n, V7X_LANES), F32), unroll=unroll)

        def update_body(j, y):
            s_new = (s_ref[j] * w_ref[t, pl.ds(j, 1), :] + sa * bv_ref[t, pl.ds(j, 1), :]
                     + v * kf_ref[t, pl.ds(j, 1), :])
            s_ref[j] = s_new
            return y + s_new * rf_ref[t, pl.ds(j, 1), :]

        yr_ref[t] = lax.fori_loop(0, n, update_body, jnp.zeros((n, V7X_LANES), F32), unroll=unroll)
        return carry

    lax.fori_loop(0, tb, step, 0)

    def finish(t, carry):
        y = yr_ref[t]
        mu = jnp.mean(y, axis=0, keepdims=True)
        yc = y - mu
        var = jnp.mean(yc * yc, axis=0, keepdims=True)
        yn = yc * lax.rsqrt(var + RWKV_GN_EPS) * lnw_ref[...] + lnb_ref[...]
        bonus = jnp.sum(rf_ref[t] * kf_ref[t] * rk_ref[...], axis=0, keepdims=True)
        y_ref[t] = (yn + bonus * v_ref[t].astype(F32)).astype(y_ref.dtype)
        return carry

    lax.fori_loop(0, tb, finish, 0, unroll=side_unroll)


def rwkv_scan(r_t, w_t, k_t, v_t, a_t, kk_t, ka_t, rk_t, lnw_t, lnb_t):
    t, n, lanes = r_t.shape
    tb = min(64, t)
    blk = pl.BlockSpec((tb, n, lanes), lambda i: (i, 0, 0))
    par = pl.BlockSpec((n, lanes), lambda i: (0, 0))
    kern = functools.partial(_rwkv_scan_kernel, tb=tb, unroll=32, side_unroll=4)
    return pl.pallas_call(
        kern,
        grid=(t // tb,),
        in_specs=[blk] * 5 + [par] * 5,
        out_specs=blk,
        out_shape=jax.ShapeDtypeStruct((t, n, lanes), BF16),
        scratch_shapes=[pltpu.VMEM((n, n, lanes), F32)] + [pltpu.VMEM((tb, n, lanes), F32)] * 5,
        compiler_params=_cparams(("arbitrary",)),
        name="rwkv_scan",
    )(r_t, w_t, k_t, v_t, a_t, kk_t, ka_t, rk_t, lnw_t, lnb_t)


def _ssd_layer(x3, norm_w, w_in, conv_w, conv_b, dt_bias, a_log, d_skip, ssd_norm_w, w_out):
    b, t, d = x3.shape
    x2 = x3.reshape(b * t, d)
    main = SSD_D_INNER + SSD_CONV_DIM
    proj = norm_matmul(x2, norm_w, w_in[:, :main].astype(BF16), BF16, main // 2)
    w_dt = _pad_cols(w_in[:, main:], V7X_LANES).astype(BF16)
    dt = norm_matmul(x2, norm_w, w_dt, F32, V7X_LANES)
    return ssd_core(proj.reshape(b, t, main), dt.reshape(b, t, V7X_LANES), x3, conv_w, conv_b,
                    dt_bias, a_log, d_skip, ssd_norm_w, w_out.astype(BF16))


def _to_lanes(z2, b, t):
    z = z2.reshape(b, t, RWKV_HEADS, RWKV_HEAD_DIM)
    return jnp.transpose(z, (1, 3, 0, 2)).reshape(t, RWKV_HEAD_DIM, b * RWKV_HEADS)


def _param_to_lanes(p, b):
    return jnp.tile(p.reshape(RWKV_HEADS, RWKV_HEAD_DIM).T, (1, b))


def _rwkv_layer(x3, norm_w, mix, w_rkv, w0, w1, w2, a0, a1, a2, g1, g2, k_k, k_a, r_k, ln_w, ln_b, w_out):
    b, t, d = x3.shape
    x2 = x3.reshape(b * t, d)
    r, w, k, v, a, g = rwkv_proj(x2, t, norm_w, mix, w_rkv, w0, w1, w2, a0, a1, a2, g1, g2)
    y_t = rwkv_scan(*(_to_lanes(z, b, t) for z in (r, w, k, v, a)),
                    *(_param_to_lanes(p, b) for p in (k_k, k_a, r_k, ln_w, ln_b)))
    y = jnp.transpose(y_t.reshape(t, RWKV_HEAD_DIM, b, RWKV_HEADS), (2, 0, 3, 1)).reshape(b * t, d)
    return gate_matmul_res(x2, y, g, w_out.astype(BF16)).reshape(b, t, d)


def _gla_layer(x3, norm_w, w_in, w_gk2, b_gk2, gla_norm_w, w_out):
    b, t, d = x3.shape
    n_pad = 2 * GLA_KEY_DIM + 2 * GLA_VALUE_DIM + V7X_LANES
    proj = norm_matmul(x3.reshape(b * t, d), norm_w, _pad_cols(w_in, n_pad).astype(BF16), BF16, n_pad)
    return gla_core(proj.reshape(b, t, n_pad), x3, w_gk2, b_gk2, gla_norm_w, w_out.astype(BF16))


def _deinterleave_heads(w_cols, heads, dk):
    dm = w_cols.shape[0]
    w4 = w_cols.reshape(dm, heads, dk // 2, 2)
    return jnp.transpose(w4, (0, 1, 3, 2)).reshape(dm, heads * dk)


def _ret_layer(x3, norm_w, w_in, w_out):
    b, t, d = x3.shape
    wq = _deinterleave_heads(w_in[:, :RET_QK_DIM], RET_HEADS, RET_DK)
    wk = _deinterleave_heads(w_in[:, RET_QK_DIM:2 * RET_QK_DIM], RET_HEADS, RET_DK)
    w_perm = jnp.concatenate([wq, wk, w_in[:, 2 * RET_QK_DIM:]], axis=1).astype(BF16)
    proj = norm_matmul(x3.reshape(b * t, d), norm_w, w_perm, BF16, w_perm.shape[1] // 2)
    inv = 1.0 / (ROPE_BASE ** jnp.linspace(0.0, 1.0, RET_DK // 2, dtype=F32))
    ang = jnp.arange(t, dtype=F32)[:, None] * inv[None]
    return ret_core(proj.reshape(b, t, w_perm.shape[1]), x3, jnp.cos(ang), jnp.sin(ang), w_out.astype(BF16))


def kernel(x, norm_mix, norm_ffn, norm_final, ssd_w_in, ssd_conv_w, ssd_conv_b, ssd_dt_bias, ssd_a_log, ssd_d, ssd_norm_w, ssd_w_out, rwkv_mix, rwkv_w_rkv, rwkv_w0, rwkv_w1, rwkv_w2, rwkv_a0, rwkv_a1, rwkv_a2, rwkv_g1, rwkv_g2, rwkv_k_k, rwkv_k_a, rwkv_r_k, rwkv_ln_w, rwkv_ln_b, rwkv_w_out, gla_w_in, gla_w_gk2, gla_b_gk2, gla_norm_w, gla_w_out, ret_w_in, ret_w_out, ffn_w_up, ffn_conv_w, ffn_conv_b, ffn_w_down):
    b, t, d = x.shape
    depth = norm_mix.shape[0]
    for i in range(depth):
        m, j = i % 4, i // 4
        if m == 0:
            x = _ssd_layer(x, norm_mix[i], ssd_w_in[j], ssd_conv_w[j], ssd_conv_b[j], ssd_dt_bias[j],
                           ssd_a_log[j], ssd_d[j], ssd_norm_w[j], ssd_w_out[j])
        elif m == 1:
            x = _rwkv_layer(x, norm_mix[i], rwkv_mix[j], rwkv_w_rkv[j], rwkv_w0[j], rwkv_w1[j], rwkv_w2[j],
                            rwkv_a0[j], rwkv_a1[j], rwkv_a2[j], rwkv_g1[j], rwkv_g2[j], rwkv_k_k[j],
                            rwkv_k_a[j], rwkv_r_k[j], rwkv_ln_w[j], rwkv_ln_b[j], rwkv_w_out[j])
        elif m == 2:
            x = _gla_layer(x, norm_mix[i], gla_w_in[j], gla_w_gk2[j], gla_b_gk2[j], gla_norm_w[j], gla_w_out[j])
        else:
            x = _ret_layer(x, norm_mix[i], ret_w_in[j], ret_w_out[j])
        x = ffn(x.reshape(b * t, d), t, norm_ffn[i], ffn_w_up[i].astype(BF16), ffn_conv_w[i], ffn_conv_b[i],
                ffn_w_down[i].astype(BF16), norm_final, final_norm=(i == depth - 1)).reshape(b, t, d)
    return x
```

```python
import functools
import math

import jax
import jax.numpy as jnp
from jax import lax
from jax.experimental import pallas as pl
from jax.experimental.pallas import tpu as pltpu

D_MODEL = 1024
NORM_EPS = 1e-5
CHUNK = 128

SSD_D_INNER = 2048
SSD_HEAD_DIM = 64
SSD_HEADS = 32
SSD_GROUPS = 8
SSD_STATE = 128
SSD_CONV = 4
SSD_CONV_DIM = SSD_D_INNER + 2 * SSD_GROUPS * SSD_STATE
SSD_HEADS_PER_GROUP = SSD_HEADS // SSD_GROUPS
SSD_GROUP_V = SSD_HEADS_PER_GROUP * SSD_HEAD_DIM

RWKV_HEADS = 16
RWKV_HEAD_DIM = 64
RWKV_GN_EPS = 64e-5

GLA_HEADS = 4
GLA_DK = 128
GLA_DV = 256
GLA_KEY_DIM = 512
GLA_VALUE_DIM = 1024
GLA_GATE_RANK = 16
GLA_GATE_NORM = 16.0
GLA_SUB = 16

RET_HEADS = 4
RET_DK = 256
RET_DV = 512
RET_QK_DIM = 1024
RET_V_DIM = 2048
ROPE_BASE = 10000.0

FFN_HIDDEN = 2816
FFN_CONV = 3

V7X_LANES = 128
V7X_BF16_SUBLANES = 16
V7X_F32_SUBLANES = 8
V7X_VMEM_LIMIT_BYTES = 56 * 1024 * 1024

F32 = jnp.float32
BF16 = jnp.bfloat16
NEG_BIG = -1e30


def _cparams(sem):
    return pltpu.CompilerParams(dimension_semantics=sem, vmem_limit_bytes=V7X_VMEM_LIMIT_BYTES)


def _core_batch(b, want):
    return max(n for n in range(1, want + 1) if b % n == 0)


def _rms(x, w):
    return x * lax.rsqrt(jnp.mean(x * x, axis=-1, keepdims=True) + NORM_EPS) * w


def _sigmoid(x):
    return 1.0 / (1.0 + jnp.exp(-x))


def _silu(x):
    return x * _sigmoid(x)


def _softplus(x):
    return jnp.maximum(x, 0.0) + jnp.log(1.0 + jnp.exp(-jnp.abs(x)))


def _dot(a, b):
    return jnp.dot(a, b, preferred_element_type=F32)


def _dot_nt(a, b):
    return lax.dot_general(a, b, (((1,), (1,)), ((), ())), preferred_element_type=F32)


def _dot_tn(a, b):
    return lax.dot_general(a, b, (((0,), (0,)), ((), ())), preferred_element_type=F32)


def _split_dot(t_bf16, x_f32):
    hi = x_f32.astype(BF16)
    lo = (x_f32 - hi.astype(F32)).astype(BF16)
    return _dot(t_bf16, hi) + _dot(t_bf16, lo)


def _tril_ones(n, dtype):
    r = lax.broadcasted_iota(jnp.int32, (n, n), 0)
    c = lax.broadcasted_iota(jnp.int32, (n, n), 1)
    return jnp.where(r >= c, 1.0, 0.0).astype(dtype)


def _norm_matmul_kernel(x_ref, nw_ref, w_ref, o_ref, xn_ref):
    @pl.when(pl.program_id(1) == 0)
    def _():
        xn_ref[...] = _rms(x_ref[...], nw_ref[...]).astype(BF16)

    o_ref[...] = _dot(xn_ref[...], w_ref[...]).astype(o_ref.dtype)


def norm_matmul(x2, norm_w, w_bf16, out_dtype, tn):
    m, d = x2.shape
    n = w_bf16.shape[1]
    tm = min(1024, m)
    return pl.pallas_call(
        _norm_matmul_kernel,
        grid=(m // tm, n // tn),
        in_specs=[
            pl.BlockSpec((tm, d), lambda i, j: (i, 0)),
            pl.BlockSpec((1, d), lambda i, j: (0, 0)),
            pl.BlockSpec((d, tn), lambda i, j: (0, j)),
        ],
        out_specs=pl.BlockSpec((tm, tn), lambda i, j: (i, j)),
        out_shape=jax.ShapeDtypeStruct((m, n), out_dtype),
        scratch_shapes=[pltpu.VMEM((tm, d), BF16)],
        compiler_params=_cparams(("parallel", "arbitrary")),
        name="norm_matmul",
    )(x2, norm_w.reshape(1, d), w_bf16)


def _gate_matmul_res_kernel(x_ref, y_ref, g_ref, w_ref, o_ref):
    k, tm = w_ref.shape[0], x_ref.shape[0]
    yg = (y_ref[...].astype(F32) * g_ref[...].astype(F32)).astype(BF16).reshape(k, tm)
    o_ref[...] = x_ref[...] + _dot_tn(yg, w_ref[...])


def gate_matmul_res(x2, batch, seq_len, y_fm, g_fm, w_bf16):
    m, d = x2.shape
    k = w_bf16.shape[0]
    tm = min(1024, seq_len)
    tiles_per_seq = seq_len // tm
    fm_spec = pl.BlockSpec((y_fm.shape[0], y_fm.shape[1] // batch, tm), lambda b, ti: (0, b, ti))
    row_spec = pl.BlockSpec((tm, d), lambda b, ti: (b * tiles_per_seq + ti, 0))
    return pl.pallas_call(
        _gate_matmul_res_kernel,
        grid=(batch, tiles_per_seq),
        in_specs=[row_spec, fm_spec, fm_spec, pl.BlockSpec((k, d), lambda b, ti: (0, 0))],
        out_specs=row_spec,
        out_shape=jax.ShapeDtypeStruct((m, d), F32),
        compiler_params=_cparams(("parallel", "parallel")),
        name="gate_matmul_res",
    )(x2, y_fm, g_fm, w_bf16)


FFN_HALO = V7X_BF16_SUBLANES


FFN_SUB = 2 * V7X_LANES
FFN_NSUB = FFN_HIDDEN // FFN_SUB


def _ffn_kernel(x_ref, xh_ref, nw_ref, wu_ref, cw_ref, cb_ref, wd_ref, fnw_ref, o_ref, xn_ref, h_ref,
                *, tm, tiles_per_seq, final_norm):
    ns = FFN_NSUB
    x = x_ref[...]
    xn_ref[pl.ds(FFN_HALO, tm), :] = _rms(x, nw_ref[...]).astype(BF16)
    keep = jnp.where(pl.program_id(0) % tiles_per_seq == 0, 0.0, 1.0)
    xn_ref[pl.ds(0, FFN_HALO), :] = (_rms(xh_ref[...], nw_ref[...]) * keep).astype(BF16)
    o_ref[...] = x

    def up(c, slot):
        h_ref[slot, 0] = _dot(xn_ref[...], wu_ref[c])
        h_ref[slot, 1] = _dot(xn_ref[...], wu_ref[ns + c])

    def conv(c, slot, part):
        cw = cw_ref[part * ns + c]
        return (cw[2:3] * h_ref[slot, part, pl.ds(FFN_HALO, tm), :]
                + cw[1:2] * h_ref[slot, part, pl.ds(FFN_HALO - 1, tm), :]
                + cw[0:1] * h_ref[slot, part, pl.ds(FFN_HALO - 2, tm), :]
                + cb_ref[part * ns + c])

    def down(c, slot):
        act = (conv(c, slot, 0) * _silu(conv(c, slot, 1))).astype(BF16)
        o_ref[...] += _dot(act, wd_ref[c])

    up(0, 0)

    def pair(p, carry):
        c = 2 * p
        up(c + 1, 1)
        down(c, 0)
        up(c + 2, 0)
        down(c + 1, 1)
        return carry

    lax.fori_loop(0, (ns - 1) // 2, pair, 0)
    down(ns - 1, 0)

    if final_norm:
        o_ref[...] = _rms(o_ref[...], fnw_ref[...])


def ffn(x2, seq_len, norm_w, w_up, conv_w, conv_b, w_down, final_norm_w, final_norm):
    m, d = x2.shape
    ns, sub = FFN_NSUB, FFN_SUB
    tm = min(1024, seq_len)
    tiles_per_seq = seq_len // tm
    halo_blocks = tm // FFN_HALO
    wu = jnp.transpose(w_up.astype(BF16).reshape(d, 2 * ns, sub), (1, 0, 2))
    cw = jnp.transpose(conv_w.reshape(FFN_CONV, 2 * ns, sub), (1, 0, 2))
    cb = conv_b.reshape(2 * ns, 1, sub)
    wd = w_down.astype(BF16).reshape(ns, sub, d)
    resident = dict(pipeline_mode=pl.Buffered(1))
    kern = functools.partial(_ffn_kernel, tm=tm, tiles_per_seq=tiles_per_seq, final_norm=final_norm)
    return pl.pallas_call(
        kern,
        grid=(m // tm,),
        in_specs=[
            pl.BlockSpec((tm, d), lambda i: (i, 0)),
            pl.BlockSpec((FFN_HALO, d), lambda i: (jnp.maximum(i * halo_blocks - 1, 0), 0)),
            pl.BlockSpec((1, d), lambda i: (0, 0)),
            pl.BlockSpec((2 * ns, d, sub), lambda i: (0, 0, 0), **resident),
            pl.BlockSpec((2 * ns, FFN_CONV, sub), lambda i: (0, 0, 0)),
            pl.BlockSpec((2 * ns, 1, sub), lambda i: (0, 0, 0)),
            pl.BlockSpec((ns, sub, d), lambda i: (0, 0, 0), **resident),
            pl.BlockSpec((1, d), lambda i: (0, 0)),
        ],
        out_specs=pl.BlockSpec((tm, d), lambda i: (i, 0)),
        out_shape=jax.ShapeDtypeStruct((m, d), F32),
        scratch_shapes=[
            pltpu.VMEM((tm + FFN_HALO, d), BF16),
            pltpu.VMEM((2, 2, tm + FFN_HALO, sub), F32),
        ],
        compiler_params=_cparams(("parallel",)),
        name="ffn",
    )(x2, x2, norm_w.reshape(1, d), wu, cw, cb, wd, final_norm_w.reshape(1, d))


SSD_HALO = V7X_BF16_SUBLANES


def _ssd_kernel(proj_ref, dt_ref, x_ref, cw_ref, cb_ref, dtb_ref, alog_ref, dexp_ref, nw_ref,
                wout_ref, e_ref, o_ref, s_ref, halo_ref, ext_ref, act_ref, yn_ref):
    @pl.when(pl.program_id(1) == 0)
    def _():
        s_ref[...] = jnp.zeros_like(s_ref)
        halo_ref[...] = jnp.zeros_like(halo_ref)

    chains = []
    for bb in range(proj_ref.shape[0]):
        one = pl.ds(bb, 1)
        chains.append(_ssd_chunk(proj_ref.at[one], dt_ref.at[one], x_ref.at[one], cw_ref, cb_ref, dtb_ref,
                                 alog_ref, dexp_ref, nw_ref, wout_ref, e_ref, o_ref.at[one], s_ref.at[bb],
                                 halo_ref.at[bb], ext_ref.at[bb], act_ref.at[bb], yn_ref.at[bb]))
    _round_robin(chains)


def _ssd_chunk(proj_ref, dt_ref, x_ref, cw_ref, cb_ref, dtb_ref, alog_ref, dexp_ref, nw_ref,
               wout_ref, e_ref, o_ref, s_ref, halo_ref, ext_ref, act_ref, yn_ref):
    L = CHUNK
    ext_ref[pl.ds(0, SSD_HALO), :] = halo_ref[...]
    ext_ref[pl.ds(SSD_HALO, L), :] = proj_ref[0, :, pl.ds(SSD_D_INNER, SSD_CONV_DIM)]
    halo_ref[...] = ext_ref[pl.ds(L, SSD_HALO), :]
    sel_r = lax.broadcasted_iota(jnp.int32, (SSD_CONV * L, L + SSD_HALO), 0)
    sel_c = lax.broadcasted_iota(jnp.int32, (SSD_CONV * L, L + SSD_HALO), 1)
    src = jnp.bitwise_and(sel_r, L - 1) + jnp.right_shift(sel_r, L.bit_length() - 1) + (SSD_HALO - SSD_CONV + 1)
    sel = jnp.where(sel_c == src, 1.0, 0.0).astype(BF16)
    panel = 512
    for p in range(SSD_CONV_DIM // panel):
        cs = pl.ds(p * panel, panel)
        shifted = _dot(sel, ext_ref[:, cs])
        acc = cb_ref[:, cs] + cw_ref[0:1, cs] * shifted[0:L]
        for tap in range(1, SSD_CONV):
            acc = acc + cw_ref[tap:tap + 1, cs] * shifted[tap * L:(tap + 1) * L]
        act_ref[:, cs] = _silu(acc)
        if p % 2 == 1:
            yield

    dt = _softplus(dt_ref[0] + dtb_ref[...])
    a = -jnp.exp(alog_ref[...])
    tril_b = _tril_ones(L, BF16)
    cum = _split_dot(tril_b, dt * a)
    cum_last = cum[L - 1:L, :]
    cum_t = cum.T
    dt_t = dt.T
    e_in = jnp.exp(cum)
    c_st = dt * jnp.exp(cum_last - cum)
    g_last = jnp.broadcast_to(jnp.exp(cum_last), (V7X_F32_SUBLANES, V7X_LANES))
    stacked = jnp.concatenate([e_in, c_st, g_last], axis=0).astype(BF16)
    expanded = _dot(stacked, e_ref[...])
    e_exp = expanded[0:L]
    c_exp = expanded[L:2 * L]
    g_exp = expanded[2 * L:2 * L + 1]
    yield

    row = lax.broadcasted_iota(jnp.int32, (L, L), 0)
    col = lax.broadcasted_iota(jnp.int32, (L, L), 1)
    causal = row >= col
    vcol = lax.broadcasted_iota(jnp.int32, (L, SSD_GROUP_V), 1) // SSD_HEAD_DIM

    for g in range(SSD_GROUPS):
        vs = pl.ds(g * SSD_GROUP_V, SSD_GROUP_V)
        kq = act_ref[:, pl.ds(SSD_D_INNER + g * SSD_STATE, SSD_STATE)].astype(BF16)
        qq = act_ref[:, pl.ds(SSD_D_INNER + (SSD_GROUPS + g) * SSD_STATE, SSD_STATE)].astype(BF16)
        xs = act_ref[:, vs]
        sb = _dot_nt(qq, kq)
        s_old = s_ref[g]
        q_s = _dot(qq, s_old.astype(BF16))
        yield
        probs = []
        vblk = []
        for hh in range(SSD_HEADS_PER_GROUP):
            h = g * SSD_HEADS_PER_GROUP + hh
            seg = cum[:, h:h + 1] - cum_t[h:h + 1, :]
            dec = jnp.exp(jnp.where(causal, seg, NEG_BIG)) * dt_t[h:h + 1, :]
            probs.append((sb * dec).astype(BF16))
            vblk.append(jnp.where(vcol == hh, xs, 0.0).astype(BF16))
        p_cat = jnp.concatenate(probs, axis=1)
        v_blk = jnp.concatenate(vblk, axis=0)
        y = _dot(p_cat, v_blk) + e_exp[:, g * SSD_GROUP_V:(g + 1) * SSD_GROUP_V] * q_s
        xs_scaled = (xs * c_exp[:, g * SSD_GROUP_V:(g + 1) * SSD_GROUP_V]).astype(BF16)
        s_ref[g] = s_old * g_exp[:, g * SSD_GROUP_V:(g + 1) * SSD_GROUP_V] + _dot_tn(kq, xs_scaled)
        yield
        y = y + dexp_ref[:, vs] * xs
        y = y * _silu(proj_ref[0, :, vs].astype(F32))
        y = y * lax.rsqrt(jnp.mean(y * y, axis=-1, keepdims=True) + NORM_EPS)
        yn_ref[:, vs] = (y * nw_ref[:, vs]).astype(BF16)
        yield

    o_ref[0] = x_ref[0] + _dot(yn_ref[...], wout_ref[...])


def ssd_core(proj3, dt3, x3, conv_w, conv_b, dt_bias, a_log, d_skip, norm_w, w_out_bf16):
    b, t, d = x3.shape
    L = CHUNK
    pad = V7X_LANES - SSD_HEADS
    dtb = jnp.pad(dt_bias, (0, pad)).reshape(1, V7X_LANES)
    alog = jnp.pad(a_log, (0, pad)).reshape(1, V7X_LANES)
    dexp = jnp.repeat(d_skip, SSD_HEAD_DIM).reshape(1, SSD_D_INNER)
    expand = (jnp.arange(V7X_LANES)[:, None] == (jnp.arange(SSD_D_INNER)[None, :] // SSD_HEAD_DIM)).astype(BF16)
    pw = proj3.shape[2]
    const = lambda bi, ci: (0, 0)
    nb = _core_batch(b, 2)
    return pl.pallas_call(
        _ssd_kernel,
        grid=(b // nb, t // L),
        in_specs=[
            pl.BlockSpec((nb, L, pw), lambda bi, ci: (bi, ci, 0)),
            pl.BlockSpec((nb, L, V7X_LANES), lambda bi, ci: (bi, ci, 0)),
            pl.BlockSpec((nb, L, d), lambda bi, ci: (bi, ci, 0)),
            pl.BlockSpec((SSD_CONV, SSD_CONV_DIM), const),
            pl.BlockSpec((1, SSD_CONV_DIM), const),
            pl.BlockSpec((1, V7X_LANES), const),
            pl.BlockSpec((1, V7X_LANES), const),
            pl.BlockSpec((1, SSD_D_INNER), const),
            pl.BlockSpec((1, SSD_D_INNER), const),
            pl.BlockSpec((SSD_D_INNER, d), const),
            pl.BlockSpec((V7X_LANES, SSD_D_INNER), const),
        ],
        out_specs=pl.BlockSpec((nb, L, d), lambda bi, ci: (bi, ci, 0)),
        out_shape=jax.ShapeDtypeStruct((b, t, d), F32),
        scratch_shapes=[
            pltpu.VMEM((nb, SSD_GROUPS, SSD_STATE, SSD_GROUP_V), F32),
            pltpu.VMEM((nb, SSD_HALO, SSD_CONV_DIM), BF16),
            pltpu.VMEM((nb, L + SSD_HALO, SSD_CONV_DIM), BF16),
            pltpu.VMEM((nb, L, SSD_CONV_DIM), F32),
            pltpu.VMEM((nb, L, SSD_D_INNER), BF16),
        ],
        compiler_params=_cparams(("parallel", "arbitrary")),
        name="ssd_core",
    )(proj3, dt3, x3, conv_w, conv_b.reshape(1, SSD_CONV_DIM), dtb, alog, dexp,
      norm_w.reshape(1, SSD_D_INNER), w_out_bf16, expand)


def _gla_kernel(proj_ref, x_ref, wgk_ref, bgk_ref, nw_ref, wout_ref, o_ref, s_ref, yn_ref):
    @pl.when(pl.program_id(1) == 0)
    def _():
        s_ref[...] = jnp.zeros_like(s_ref)

    chains = []
    for bb in range(proj_ref.shape[0]):
        one = pl.ds(bb, 1)
        chains.append(_gla_chunk(proj_ref.at[one], x_ref.at[one], wgk_ref, bgk_ref, nw_ref, wout_ref,
                                 o_ref.at[one], s_ref.at[bb], yn_ref.at[bb]))
    _round_robin(chains)


def _round_robin(chains):
    while chains:
        chains = [c for c in chains if next(c, StopIteration) is not StopIteration]


def _gla_chunk(proj_ref, x_ref, wgk_ref, bgk_ref, nw_ref, wout_ref, o_ref, s_ref, yn_ref):
    L = CHUNK
    gk_off = 2 * GLA_KEY_DIM + 2 * GLA_VALUE_DIM
    u = _dot(proj_ref[0, :, pl.ds(gk_off, V7X_LANES)], wgk_ref[...]) + bgk_ref[...]
    yield
    log_a = (jnp.minimum(u, 0.0) - jnp.log(1.0 + jnp.exp(-jnp.abs(u)))) * (1.0 / GLA_GATE_NORM)
    gcum = _split_dot(_tril_ones(L, BF16), log_a)
    yield

    row = lax.broadcasted_iota(jnp.int32, (L, L), 0)
    col = lax.broadcasted_iota(jnp.int32, (L, L), 1)
    causal = row >= col
    krow = lax.broadcasted_iota(jnp.int32, (L, GLA_DK), 0)
    nsub = L // GLA_SUB

    for h in range(GLA_HEADS):
        ks = pl.ds(h * GLA_DK, GLA_DK)
        q = proj_ref[0, :, ks].astype(F32) * (GLA_DK ** -0.5)
        k = proj_ref[0, :, pl.ds(GLA_KEY_DIM + h * GLA_DK, GLA_DK)].astype(F32)
        v = proj_ref[0, :, pl.ds(2 * GLA_KEY_DIM + h * GLA_DV, GLA_DV)]
        gh = gcum[:, h * GLA_DK:(h + 1) * GLA_DK]
        blocks = []
        for i in range(nsub):
            r0 = i * GLA_SUB
            if i == 0:
                g_ref_row = jnp.zeros((1, GLA_DK), F32)
            else:
                g_ref_row = gh[r0 - 1:r0, :]
            q_i = (q[r0:r0 + GLA_SUB] * jnp.exp(gh[r0:r0 + GLA_SUB] - g_ref_row)).astype(BF16)
            k_i = (k * jnp.exp(jnp.where(krow < r0 + GLA_SUB, g_ref_row - gh, NEG_BIG))).astype(BF16)
            blocks.append(_dot_nt(q_i, k_i))
        yield
        scores = jnp.where(causal, jnp.concatenate(blocks, axis=0), 0.0).astype(BF16)
        st_old = s_ref[h]
        o = _dot(scores, v) + _dot_nt((q * jnp.exp(gh)).astype(BF16), st_old.astype(BF16))
        g_last = gh[L - 1:L, :]
        k_hat = (k * jnp.exp(g_last - gh)).astype(BF16)
        s_ref[h] = st_old * jnp.exp(g_last) + _dot_tn(v, k_hat)
        yield
        o = o * lax.rsqrt(jnp.mean(o * o, axis=-1, keepdims=True) + NORM_EPS) * nw_ref[...]
        gate = proj_ref[0, :, pl.ds(2 * GLA_KEY_DIM + GLA_VALUE_DIM + h * GLA_DV, GLA_DV)].astype(F32)
        yn_ref[:, pl.ds(h * GLA_DV, GLA_DV)] = (o * _silu(gate)).astype(BF16)
        yield

    o_ref[0] = x_ref[0] + _dot(yn_ref[...], wout_ref[...])


def gla_core(proj3, x3, w_gk2, b_gk2, norm_w, w_out_bf16):
    b, t, d = x3.shape
    L = CHUNK
    pw = proj3.shape[2]
    wgk = jnp.pad(w_gk2, ((0, V7X_LANES - GLA_GATE_RANK), (0, 0))).astype(BF16)
    const = lambda bi, ci: (0, 0)
    nb = _core_batch(b, 4)
    return pl.pallas_call(
        _gla_kernel,
        grid=(b // nb, t // L),
        in_specs=[
            pl.BlockSpec((nb, L, pw), lambda bi, ci: (bi, ci, 0)),
            pl.BlockSpec((nb, L, d), lambda bi, ci: (bi, ci, 0)),
            pl.BlockSpec((V7X_LANES, GLA_KEY_DIM), const),
            pl.BlockSpec((1, GLA_KEY_DIM), const),
            pl.BlockSpec((1, GLA_DV), const),
            pl.BlockSpec((GLA_VALUE_DIM, d), const),
        ],
        out_specs=pl.BlockSpec((nb, L, d), lambda bi, ci: (bi, ci, 0)),
        out_shape=jax.ShapeDtypeStruct((b, t, d), F32),
        scratch_shapes=[
            pltpu.VMEM((nb, GLA_HEADS, GLA_DV, GLA_DK), F32),
            pltpu.VMEM((nb, L, GLA_VALUE_DIM), BF16),
        ],
        compiler_params=_cparams(("parallel", "arbitrary")),
        name="gla_core",
    )(proj3, x3, wgk, b_gk2.reshape(1, GLA_KEY_DIM), norm_w.reshape(1, GLA_DV), w_out_bf16)


def _ret_kernel(proj_ref, x_ref, cos_ref, sin_ref, wout_ref, o_ref, s_ref, yn_ref):
    @pl.when(pl.program_id(1) == 0)
    def _():
        s_ref[...] = jnp.zeros_like(s_ref)

    chains = []
    for bb in range(proj_ref.shape[0]):
        one = pl.ds(bb, 1)
        chains.append(_ret_chunk(proj_ref.at[one], x_ref.at[one], cos_ref, sin_ref, wout_ref, o_ref.at[one],
                                 s_ref.at[bb], yn_ref.at[bb]))
    _round_robin(chains)


def _ret_chunk(proj_ref, x_ref, cos_ref, sin_ref, wout_ref, o_ref, s_ref, yn_ref):
    L = CHUNK
    cos = cos_ref[...]
    sin = sin_ref[...]
    half = RET_DK // 2

    def rotary(xh):
        xe = xh[:, :half]
        xo = xh[:, half:]
        return jnp.concatenate([xe * cos - xo * sin, xo * cos + xe * sin], axis=1)

    rowf = lax.broadcasted_iota(jnp.int32, (L, L), 0).astype(F32)
    colf = lax.broadcasted_iota(jnp.int32, (L, L), 1).astype(F32)
    diff = rowf - colf
    trow = lax.broadcasted_iota(jnp.int32, (L, V7X_LANES), 0).astype(F32)

    for h in range(RET_HEADS):
        lg = math.log(1.0 - 2.0 ** (-5.0 - h))
        q = rotary(proj_ref[0, :, pl.ds(h * RET_DK, RET_DK)].astype(F32))
        k = rotary(proj_ref[0, :, pl.ds(RET_QK_DIM + h * RET_DK, RET_DK)].astype(F32)) * (RET_DK ** -0.5)
        v = proj_ref[0, :, pl.ds(2 * RET_QK_DIM + h * RET_DV, RET_DV)]
        qb = q.astype(BF16)
        dec = jnp.exp(jnp.where(diff >= 0.0, diff * lg, NEG_BIG))
        p = (_dot_nt(qb, k.astype(BF16)) * dec).astype(BF16)
        yield
        e_in = jnp.exp((trow + 1.0) * lg)
        e_in = jnp.concatenate([e_in] * (RET_DV // V7X_LANES), axis=1)
        s_old = s_ref[h]
        o = _dot(p, v) + e_in * _dot(qb, s_old.astype(BF16))
        e_out = jnp.exp((float(L - 1) - trow) * lg)
        k_hat = (k * jnp.concatenate([e_out] * (RET_DK // V7X_LANES), axis=1)).astype(BF16)
        s_ref[h] = s_old * math.exp(L * lg) + _dot_tn(k_hat, v)
        yield
        o = o * lax.rsqrt(jnp.mean(o * o, axis=-1, keepdims=True) + NORM_EPS)
        gate = proj_ref[0, :, pl.ds(2 * RET_QK_DIM + RET_V_DIM + h * RET_DV, RET_DV)].astype(F32)
        yn_ref[:, pl.ds(h * RET_DV, RET_DV)] = (_silu(gate) * o).astype(BF16)
        yield

    o_ref[0] = x_ref[0] + _dot(yn_ref[...], wout_ref[...])


def ret_core(proj3, x3, cos, sin, w_out_bf16):
    b, t, d = x3.shape
    L = CHUNK
    pw = proj3.shape[2]
    nb = _core_batch(b, 4)
    return pl.pallas_call(
        _ret_kernel,
        grid=(b // nb, t // L),
        in_specs=[
            pl.BlockSpec((nb, L, pw), lambda bi, ci: (bi, ci, 0)),
            pl.BlockSpec((nb, L, d), lambda bi, ci: (bi, ci, 0)),
            pl.BlockSpec((L, RET_DK // 2), lambda bi, ci: (ci, 0)),
            pl.BlockSpec((L, RET_DK // 2), lambda bi, ci: (ci, 0)),
            pl.BlockSpec((RET_V_DIM, d), lambda bi, ci: (0, 0)),
        ],
        out_specs=pl.BlockSpec((nb, L, d), lambda bi, ci: (bi, ci, 0)),
        out_shape=jax.ShapeDtypeStruct((b, t, d), F32),
        scratch_shapes=[
            pltpu.VMEM((nb, RET_HEADS, RET_DK, RET_DV), F32),
            pltpu.VMEM((nb, L, RET_V_DIM), BF16),
        ],
        compiler_params=_cparams(("parallel", "arbitrary")),
        name="ret_core",
    )(proj3, x3, cos, sin, w_out_bf16)


RWKV_HALO = V7X_F32_SUBLANES


def _rwkv_mixed_inputs(x_ref, xh_ref, nw_ref, mix_ref, hs_ref, tm):
    h = _rms(x_ref[...], nw_ref[...])
    keep = jnp.where(pl.program_id(1) == 0, 0.0, 1.0)
    hs_ref[pl.ds(0, RWKV_HALO), :] = _rms(xh_ref[...], nw_ref[...]) * keep
    hs_ref[pl.ds(RWKV_HALO, tm), :] = h
    xx = hs_ref[pl.ds(RWKV_HALO - 1, tm), :] - h
    return lambda idx: (h + xx * mix_ref[idx:idx + 1, :]).astype(BF16)


def _emit_feature_major(o_ref, val):
    o_ref[...] = val.reshape(o_ref.shape).astype(o_ref.dtype)


def _rwkv_proj_kernel(x_ref, xh_ref, nw_ref, mix_ref, wr_ref, wk_ref, wv_ref, w0_ref, w1_ref, w2_ref,
                      a0_ref, a1_ref, a2_ref, g1_ref, g2_ref,
                      r_ref, w_ref, k_ref, v_ref, a_ref, g_ref, hs_ref, *, tm):
    mixed = _rwkv_mixed_inputs(x_ref, xh_ref, nw_ref, mix_ref, hs_ref, tm)

    def lanes(col_ref):
        return jnp.concatenate([col_ref[...]] * (tm // V7X_LANES), axis=1)

    hid_w = jnp.tanh(_dot(mixed(1), w1_ref[...])).astype(BF16)
    _emit_feature_major(r_ref, _dot_nt(wr_ref[...], mixed(0)))
    w_raw = -_softplus(-(lanes(w0_ref) + _dot_nt(w2_ref[...], hid_w))) - 0.5
    _emit_feature_major(w_ref, jnp.exp(-jnp.exp(w_raw)))
    hid_a = _dot(mixed(4), a1_ref[...]).astype(BF16)
    _emit_feature_major(k_ref, _dot_nt(wk_ref[...], mixed(2)))
    _emit_feature_major(a_ref, _sigmoid(lanes(a0_ref) + _dot_nt(a2_ref[...], hid_a)))
    _emit_feature_major(v_ref, _dot_nt(wv_ref[...], mixed(3)))
    hid_g = _sigmoid(_dot(mixed(5), g1_ref[...])).astype(BF16)
    _emit_feature_major(g_ref, _dot_nt(g2_ref[...], hid_g))


def _pad_cols(w, n):
    return jnp.pad(w, ((0, 0), (0, n - w.shape[1])))


def _pad_rows(w, n):
    return jnp.pad(w, ((0, n - w.shape[0]), (0, 0)))


def _head_minor(w):
    lead = w.shape[:-1]
    w3 = w.reshape(*lead, RWKV_HEADS, RWKV_HEAD_DIM)
    return jnp.swapaxes(w3, -1, -2).reshape(*lead, RWKV_HEADS * RWKV_HEAD_DIM)


def rwkv_proj(x2, batch, seq_len, norm_w, mix, w_rkv, w0, w1, w2, a0, a1, a2, g1, g2):
    m, d = x2.shape
    tm = min(512, seq_len)
    tiles_per_seq = seq_len // tm
    halo_blocks = tm // RWKV_HALO
    lw = V7X_LANES
    lg = 2 * V7X_LANES
    wr_t, wk_t, wv_t = (_head_minor(w_rkv[n]).T.astype(BF16) for n in range(3))
    w1p, w2_t = _pad_cols(w1, lw).astype(BF16), _head_minor(_pad_rows(w2, lw)).T.astype(BF16)
    a1p, a2_t = _pad_cols(a1, lw).astype(BF16), _head_minor(_pad_rows(a2, lw)).T.astype(BF16)
    g1p, g2_t = _pad_cols(g1, lg).astype(BF16), _head_minor(_pad_rows(g2, lg)).T.astype(BF16)
    w0c = jnp.broadcast_to(_head_minor(w0)[:, None], (d, V7X_LANES))
    a0c = jnp.broadcast_to(_head_minor(a0)[:, None], (d, V7X_LANES))
    row = lambda b, ti: (b * tiles_per_seq + ti, 0)
    halo = lambda b, ti: (jnp.maximum((b * tiles_per_seq + ti) * halo_blocks - 1, 0), 0)
    const = lambda b, ti: (0, 0)
    out_spec = pl.BlockSpec((RWKV_HEAD_DIM, RWKV_HEADS, tm), lambda b, ti: (0, b, ti))
    out_sd = lambda dt: jax.ShapeDtypeStruct((RWKV_HEAD_DIM, batch * RWKV_HEADS, seq_len), dt)
    return pl.pallas_call(
        functools.partial(_rwkv_proj_kernel, tm=tm),
        grid=(batch, tiles_per_seq),
        in_specs=[
            pl.BlockSpec((tm, d), row), pl.BlockSpec((RWKV_HALO, d), halo),
            pl.BlockSpec((1, d), const), pl.BlockSpec((6, d), const),
            pl.BlockSpec((d, d), const), pl.BlockSpec((d, d), const), pl.BlockSpec((d, d), const),
            pl.BlockSpec((d, V7X_LANES), const), pl.BlockSpec((d, lw), const), pl.BlockSpec((d, lw), const),
            pl.BlockSpec((d, V7X_LANES), const), pl.BlockSpec((d, lw), const), pl.BlockSpec((d, lw), const),
            pl.BlockSpec((d, lg), const), pl.BlockSpec((d, lg), const),
        ],
        out_specs=[out_spec] * 6,
        out_shape=[out_sd(BF16), out_sd(F32), out_sd(BF16), out_sd(BF16), out_sd(BF16), out_sd(BF16)],
        scratch_shapes=[pltpu.VMEM((tm + RWKV_HALO, d), F32)],
        compiler_params=_cparams(("parallel", "parallel")),
        name="rwkv_proj",
    )(x2, x2, norm_w.reshape(1, d), mix, wr_t, wk_t, wv_t, w0c, w1p, w2_t, a0c, a1p, a2_t, g1p, g2_t)


def _rwkv_scan_kernel(r_ref, w_ref, k_ref, a_ref, v_ref, kk_ref, ka_ref, rk_ref, lnw_ref, lnb_ref,
                      y_ref, s_ref, wz_ref, av_ref, bv_ref, kf_ref, rf_ref, yr_ref, bon_ref, *, tb, unroll,
                      side_unroll):
    n = RWKV_HEAD_DIM
    nl = V7X_LANES

    @pl.when(pl.program_id(0) == 0)
    def _():
        s_ref[...] = jnp.zeros_like(s_ref)

    def col(ref, j):
        return ref[:, pl.ds(j * nl, nl)].astype(F32)

    norm2 = jnp.zeros((tb, nl), F32)
    for j in range(n):
        kk = col(k_ref, j) * kk_ref[:, pl.ds(j * nl, nl)]
        norm2 = norm2 + kk * kk
    inv = 1.0 / jnp.maximum(jnp.sqrt(norm2), 1e-12)
    bonus = jnp.zeros((tb, nl), F32)
    for j in range(n):
        cs = pl.ds(j * nl, nl)
        k = col(k_ref, j)
        a = col(a_ref, j)
        r = col(r_ref, j)
        kk = k * kk_ref[:, cs] * inv
        kf = k * (1.0 + (a - 1.0) * ka_ref[:, cs])
        av_ref[j] = -kk
        bv_ref[j] = kk * a
        kf_ref[j] = kf
        rf_ref[j] = r
        wz_ref[j] = w_ref[:, cs]
        bonus = bonus + r * kf * rk_ref[:, cs]
    bon_ref[...] = bonus

    def step(t, carry):
        v = v_ref[t].astype(F32)
        row = pl.ds(t, 1)

        def sa_body(j, sa):
            return sa + s_ref[j] * av_ref[j, row, :]

        sa = lax.fori_loop(0, n, sa_body, jnp.zeros((n, nl), F32), unroll=unroll)

        def update_body(j, y):
            s_new = s_ref[j] * wz_ref[j, row, :] + sa * bv_ref[j, row, :] + v * kf_ref[j, row, :]
            s_ref[j] = s_new
            return y + s_new * rf_ref[j, row, :]

        yr_ref[t] = lax.fori_loop(0, n, update_body, jnp.zeros((n, nl), F32), unroll=unroll)
        return carry

    lax.fori_loop(0, tb, step, 0)

    def finish(t, carry):
        y = yr_ref[t]
        mu = jnp.mean(y, axis=0, keepdims=True)
        yc = y - mu
        var = jnp.mean(yc * yc, axis=0, keepdims=True)
        yn = yc * lax.rsqrt(var + RWKV_GN_EPS) * lnw_ref[...] + lnb_ref[...]
        y_ref[t] = (yn + bon_ref[pl.ds(t, 1), :] * v_ref[t].astype(F32)).astype(y_ref.dtype)
        return carry

    lax.fori_loop(0, tb, finish, 0, unroll=side_unroll)


def rwkv_scan(r_t, w_t, k_t, a_t, v_t, kk_t, ka_t, rk_t, lnw_t, lnb_t):
    t, n, lanes = v_t.shape
    tb = min(64, t)
    flat = pl.BlockSpec((tb, n * lanes), lambda i: (i, 0))
    blk = pl.BlockSpec((tb, n, lanes), lambda i: (i, 0, 0))
    kpar = pl.BlockSpec((1, n * lanes), lambda i: (0, 0))
    ipar = pl.BlockSpec((n, lanes), lambda i: (0, 0))
    kern = functools.partial(_rwkv_scan_kernel, tb=tb, unroll=32, side_unroll=4)
    per_key = pltpu.VMEM((n, tb, lanes), F32)
    return pl.pallas_call(
        kern,
        grid=(t // tb,),
        in_specs=[flat] * 4 + [blk] + [kpar] * 3 + [ipar] * 2,
        out_specs=blk,
        out_shape=jax.ShapeDtypeStruct((t, n, lanes), BF16),
        scratch_shapes=[pltpu.VMEM((n, n, lanes), F32)] + [per_key] * 5
                       + [pltpu.VMEM((tb, n, lanes), F32), pltpu.VMEM((tb, lanes), F32)],
        compiler_params=_cparams(("arbitrary",)),
        name="rwkv_scan",
    )(r_t, w_t, k_t, a_t, v_t, kk_t, ka_t, rk_t, lnw_t, lnb_t)


def _ssd_layer(x3, norm_w, w_in, conv_w, conv_b, dt_bias, a_log, d_skip, ssd_norm_w, w_out):
    b, t, d = x3.shape
    x2 = x3.reshape(b * t, d)
    main = SSD_D_INNER + SSD_CONV_DIM
    proj = norm_matmul(x2, norm_w, w_in[:, :main].astype(BF16), BF16, main // 2)
    w_dt = _pad_cols(w_in[:, main:], V7X_LANES).astype(BF16)
    dt = norm_matmul(x2, norm_w, w_dt, F32, V7X_LANES)
    return ssd_core(proj.reshape(b, t, main), dt.reshape(b, t, V7X_LANES), x3, conv_w, conv_b,
                    dt_bias, a_log, d_skip, ssd_norm_w, w_out.astype(BF16))


def _to_time_major(z_fm):
    n, lanes, t = z_fm.shape
    return z_fm.reshape(n * lanes, t).T


def _param_to_lanes(p, b):
    return jnp.tile(p.reshape(RWKV_HEADS, RWKV_HEAD_DIM).T, (1, b))


def _rwkv_layer(x3, norm_w, mix, w_rkv, w0, w1, w2, a0, a1, a2, g1, g2, k_k, k_a, r_k, ln_w, ln_b, w_out):
    b, t, d = x3.shape
    x2 = x3.reshape(b * t, d)
    r, w, k, v, a, g = rwkv_proj(x2, b, t, norm_w, mix, w_rkv, w0, w1, w2, a0, a1, a2, g1, g2)
    lanes = b * RWKV_HEADS
    y_t = rwkv_scan(*(_to_time_major(z) for z in (r, w, k, a)),
                    _to_time_major(v).reshape(t, RWKV_HEAD_DIM, lanes),
                    *(_param_to_lanes(p, b).reshape(1, RWKV_HEAD_DIM * lanes) for p in (k_k, k_a, r_k)),
                    *(_param_to_lanes(p, b) for p in (ln_w, ln_b)))
    n, lanes = y_t.shape[1:]
    y_fm = y_t.reshape(t, n * lanes).T.reshape(n, lanes, t)
    w_out_hm = jnp.swapaxes(w_out.reshape(RWKV_HEADS, RWKV_HEAD_DIM, d), 0, 1).reshape(d, d)
    return gate_matmul_res(x2, b, t, y_fm, g, w_out_hm.astype(BF16)).reshape(b, t, d)


def _gla_layer(x3, norm_w, w_in, w_gk2, b_gk2, gla_norm_w, w_out):
    b, t, d = x3.shape
    n_pad = 2 * GLA_KEY_DIM + 2 * GLA_VALUE_DIM + V7X_LANES
    proj = norm_matmul(x3.reshape(b * t, d), norm_w, _pad_cols(w_in, n_pad).astype(BF16), BF16, n_pad)
    return gla_core(proj.reshape(b, t, n_pad), x3, w_gk2, b_gk2, gla_norm_w, w_out.astype(BF16))


def _deinterleave_heads(w_cols, heads, dk):
    dm = w_cols.shape[0]
    w4 = w_cols.reshape(dm, heads, dk // 2, 2)
    return jnp.transpose(w4, (0, 1, 3, 2)).reshape(dm, heads * dk)


def _ret_layer(x3, norm_w, w_in, w_out):
    b, t, d = x3.shape
    wq = _deinterleave_heads(w_in[:, :RET_QK_DIM], RET_HEADS, RET_DK)
    wk = _deinterleave_heads(w_in[:, RET_QK_DIM:2 * RET_QK_DIM], RET_HEADS, RET_DK)
    w_perm = jnp.concatenate([wq, wk, w_in[:, 2 * RET_QK_DIM:]], axis=1).astype(BF16)
    proj = norm_matmul(x3.reshape(b * t, d), norm_w, w_perm, BF16, w_perm.shape[1] // 2)
    inv = 1.0 / (ROPE_BASE ** jnp.linspace(0.0, 1.0, RET_DK // 2, dtype=F32))
    ang = jnp.arange(t, dtype=F32)[:, None] * inv[None]
    return ret_core(proj.reshape(b, t, w_perm.shape[1]), x3, jnp.cos(ang), jnp.sin(ang), w_out.astype(BF16))


def kernel(x, norm_mix, norm_ffn, norm_final, ssd_w_in, ssd_conv_w, ssd_conv_b, ssd_dt_bias, ssd_a_log, ssd_d, ssd_norm_w, ssd_w_out, rwkv_mix, rwkv_w_rkv, rwkv_w0, rwkv_w1, rwkv_w2, rwkv_a0, rwkv_a1, rwkv_a2, rwkv_g1, rwkv_g2, rwkv_k_k, rwkv_k_a, rwkv_r_k, rwkv_ln_w, rwkv_ln_b, rwkv_w_out, gla_w_in, gla_w_gk2, gla_b_gk2, gla_norm_w, gla_w_out, ret_w_in, ret_w_out, ffn_w_up, ffn_conv_w, ffn_conv_b, ffn_w_down):
    b, t, d = x.shape
    depth = norm_mix.shape[0]
    for i in range(depth):
        m, j = i % 4, i // 4
        if m == 0:
            x = _ssd_layer(x, norm_mix[i], ssd_w_in[j], ssd_conv_w[j], ssd_conv_b[j], ssd_dt_bias[j],
                           ssd_a_log[j], ssd_d[j], ssd_norm_w[j], ssd_w_out[j])
        elif m == 1:
            x = _rwkv_layer(x, norm_mix[i], rwkv_mix[j], rwkv_w_rkv[j], rwkv_w0[j], rwkv_w1[j], rwkv_w2[j],
                            rwkv_a0[j], rwkv_a1[j], rwkv_a2[j], rwkv_g1[j], rwkv_g2[j], rwkv_k_k[j],
                            rwkv_k_a[j], rwkv_r_k[j], rwkv_ln_w[j], rwkv_ln_b[j], rwkv_w_out[j])
        elif m == 2:
            x = _gla_layer(x, norm_mix[i], gla_w_in[j], gla_w_gk2[j], gla_b_gk2[j], gla_norm_w[j], gla_w_out[j])
        else:
            x = _ret_layer(x, norm_mix[i], ret_w_in[j], ret_w_out[j])
        x = ffn(x.reshape(b * t, d), t, norm_ffn[i], ffn_w_up[i].astype(BF16), ffn_conv_w[i], ffn_conv_b[i],
                ffn_w_down[i].astype(BF16), norm_final, final_norm=(i == depth - 1)).reshape(b, t, d)
    return x
```
